```python
import math
import jax, jax.numpy as jnp
from jax import lax
import numpy as np

D_MODEL = 1024
BATCH = 32
SEQ = 256
DEPTH = 2
DEC_BATCH = 8
DEC_SEQ = 1024
PAST_LEN = 512

GRID_W = 64
POS_BASE = 10000.0
HEAD_DIM = 64
MIX_W = D_MODEL
GLA_HEADS = D_MODEL // 4 // HEAD_DIM
MLSTM_HEADS = D_MODEL // 4 // HEAD_DIM
SSD_HEADS = D_MODEL // 2 // HEAD_DIM
N_MIX_HEADS = GLA_HEADS + MLSTM_HEADS + SSD_HEADS
GLA_W = GLA_HEADS * HEAD_DIM
MLSTM_W = MLSTM_HEADS * HEAD_DIM
SSD_W = SSD_HEADS * HEAD_DIM
GLA_DK = HEAD_DIM // 2
GLA_QK = GLA_HEADS * GLA_DK
GLA_RANK = 16
GLA_GATE_TEMP = 16.0
SSD_GROUPS = 2
SSD_HPG = SSD_HEADS // SSD_GROUPS
SSD_N = 64
SSD_GN = SSD_GROUPS * SSD_N
CONV_K = 3
CONV_CH = SSD_W + 2 * SSD_GN
CHUNK = 64
PEER_HEADS = 8
PEER_DK = 128
N_KEYS = 128
N_EXPERTS = N_KEYS * N_KEYS
PEER_TOPK = 16
PEER_BLOCK = 128
DEEPNORM_ALPHA = (2 * DEPTH) ** 0.25
DEEPNORM_BETA = (8 * DEPTH) ** -0.25
IN_SPLITS = (GLA_QK, GLA_QK, GLA_W, GLA_W, 2 * GLA_RANK,
             MLSTM_W, MLSTM_W, MLSTM_W, MLSTM_W, 2 * MLSTM_HEADS, 2 * MLSTM_HEADS,
             SSD_W, SSD_W, SSD_GN, SSD_GN, 2 * SSD_HEADS)
IN_W = (2 * GLA_QK + 2 * GLA_W + 2 * GLA_RANK + 4 * MLSTM_W + 4 * MLSTM_HEADS
        + 2 * SSD_W + 2 * SSD_GN + 2 * SSD_HEADS)

kernel_name = "hybrid_gla_mlstm_ssd_peer_flow_step"


def _ln(x, eps=1e-6):
    xf = x.astype(jnp.float32)
    mu = jnp.mean(xf, axis=-1, keepdims=True)
    var = jnp.mean(jnp.square(xf - mu), axis=-1, keepdims=True)
    return ((xf - mu) * lax.rsqrt(var + eps)).astype(x.dtype)


def _rms_norm(x, eps=1e-6):
    xf = x.astype(jnp.float32)
    return (xf * lax.rsqrt(jnp.mean(jnp.square(xf), axis=-1, keepdims=True) + eps)).astype(x.dtype)


def _flip(t):
    return jnp.flip(t, axis=1)


def _chunks(t):
    b, l = t.shape[:2]
    return jnp.moveaxis(t.reshape((b, l // CHUNK, CHUNK) + t.shape[2:]), 1, 0)


def _unchunk(t):
    n, b, c = t.shape[:3]
    return jnp.moveaxis(t, 0, 1).reshape((b, n * c) + t.shape[3:])


def _lower_mask():
    return jnp.tril(jnp.ones((CHUNK, CHUNK), dtype=bool))


def _grid_pos_embed(n_tokens, dim):
    rows = n_tokens // GRID_W
    r = jnp.repeat(jnp.arange(rows, dtype=jnp.float32), GRID_W)
    col = jnp.tile(jnp.arange(GRID_W, dtype=jnp.float32), rows)
    quarter = dim // 4
    inv_freq = 1.0 / (POS_BASE ** (jnp.arange(quarter, dtype=jnp.float32) / quarter))
    ang_r = r[:, None] * inv_freq
    ang_c = col[:, None] * inv_freq
    return jnp.concatenate([jnp.sin(ang_r), jnp.cos(ang_r), jnp.sin(ang_c), jnp.cos(ang_c)], axis=-1)


def _centered_dwconv(x, w, b):
    y = lax.conv_general_dilated(x, w[:, None, :], window_strides=(1,),
                                 padding=[(CONV_K // 2, CONV_K // 2)],
                                 dimension_numbers=('NWC', 'WIO', 'NWC'),
                                 feature_group_count=x.shape[-1])
    return y + b


def _gla_scan(q, k, v, log_a, s0):
    out_dtype = v.dtype
    mask = _lower_mask()[None, :, :, None, None]

    def step(s, inp):
        qc, kc, vc, lac = inp
        b = jnp.cumsum(lac, axis=1)
        rel = jnp.exp(jnp.where(mask, b[:, :, None] - b[:, None], -jnp.inf))
        att = jnp.einsum('bthd,bshd,btshd->bhts', qc, kc, rel)
        o = (jnp.einsum('bhts,bshv->bthv', att, vc)
             + jnp.einsum('bthd,bhdv->bthv', qc * jnp.exp(b), s))
        b_last = b[:, -1]
        s = (jnp.exp(b_last)[..., None] * s
             + jnp.einsum('bshd,bshv->bhdv', kc * jnp.exp(b_last[:, None] - b), vc))
        return s, o

    xs = tuple(_chunks(t.astype(jnp.float32)) for t in (q, k, v, log_a))
    s_fin, o = lax.scan(step, s0.astype(jnp.float32), xs)
    return _unchunk(o).astype(out_dtype), s_fin.astype(out_dtype)


def _mlstm_scan(q, k, v, i_pre, log_f, c0, n0, m0):
    out_dtype = v.dtype
    mask = _lower_mask()[None, :, :, None]

    def step(carry, inp):
        c, n, m = carry
        qc, kc, vc, ic, fc = inp
        fcum = jnp.cumsum(fc, axis=1)
        dmat = jnp.where(mask, fcum[:, :, None] - fcum[:, None] + ic[:, None], -jnp.inf)
        m_t = jnp.maximum(fcum + m[:, None], jnp.max(dmat, axis=2))
        w_state = jnp.exp(fcum + m[:, None] - m_t)
        w = jnp.exp(dmat - m_t[:, :, None]) * jnp.einsum('bthd,bshd->btsh', qc, kc)
        num = (w_state[..., None] * jnp.einsum('bhvd,bthd->bthv', c, qc)
               + jnp.einsum('btsh,bshv->bthv', w, vc))
        den = w_state * jnp.einsum('bhd,bthd->bth', n, qc) + jnp.sum(w, axis=2)
        h = num / jnp.maximum(jnp.abs(den), jnp.exp(-m_t))[..., None]
        f_tot = fcum[:, -1]
        d_last = f_tot[:, None] - fcum + ic
        m_new = jnp.maximum(f_tot + m, jnp.max(d_last, axis=1))
        w_s = jnp.exp(d_last - m_new[:, None])
        keep = jnp.exp(f_tot + m - m_new)
        c = keep[..., None, None] * c + jnp.einsum('bsh,bshv,bshd->bhvd', w_s, vc, kc)
        n = keep[..., None] * n + jnp.einsum('bsh,bshd->bhd', w_s, kc)
        return (c, n, m_new), h

    xs = tuple(_chunks(t.astype(jnp.float32)) for t in (q, k, v, i_pre, log_f))
    init = (c0.astype(jnp.float32), n0.astype(jnp.float32), m0.astype(jnp.float32))
    (c_f, n_f, m_f), h = lax.scan(step, init, xs)
    return (_unchunk(h).astype(out_dtype), c_f.astype(out_dtype),
            n_f.astype(out_dtype), m_f.astype(out_dtype))


def _ssd_scan(x, dt, a, bm, cm, s0):
    out_dtype = x.dtype
    b_, l = x.shape[:2]
    f32 = jnp.float32
    xg = x.astype(f32).reshape(b_, l, SSD_GROUPS, SSD_HPG, HEAD_DIM)
    dtg = dt.astype(f32).reshape(b_, l, SSD_GROUPS, SSD_HPG)
    lag = dtg * a.astype(f32).reshape(SSD_GROUPS, SSD_HPG)
    mask = _lower_mask()[None, :, :, None, None]

    def step(s, inp):
        xc, dtc, lac, bc, cc = inp
        acum = jnp.cumsum(lac, axis=1)
        decay = jnp.exp(jnp.where(mask, acum[:, :, None] - acum[:, None], -jnp.inf))
        cb = jnp.einsum('btgn,bsgn->btsg', cc, bc)
        y = (jnp.einsum('btsg,btsgr,bsgrp->btgrp', cb, decay, xc * dtc[..., None])
             + jnp.exp(acum)[..., None] * jnp.einsum('bgrpn,btgn->btgrp', s, cc))
        a_last = acum[:, -1]
        s = (jnp.exp(a_last)[..., None, None] * s
             + jnp.einsum('bsgr,bsgrp,bsgn->bgrpn', jnp.exp(a_last[:, None] - acum) * dtc, xc, bc))
        return s, y

    xs = (_chunks(xg), _chunks(dtg), _chunks(lag), _chunks(bm.astype(f32)), _chunks(cm.astype(f32)))
    s_init = s0.astype(f32).reshape(b_, SSD_GROUPS, SSD_HPG, HEAD_DIM, SSD_N)
    s_fin, y = lax.scan(step, s_init, xs)
    y = _unchunk(y).reshape(b_, l, SSD_HEADS, HEAD_DIM)
    return y.astype(out_dtype), s_fin.reshape(b_, SSD_HEADS, HEAD_DIM, SSD_N).astype(out_dtype)


def _mixer(u, st, w_in, conv_w, conv_b, gla_wa2, gla_ba, mlstm_bi, mlstm_bf,
           ssd_dt_bias, ssd_a_log, ssd_d, gnorm_w, w_out):
    st_gla, st_mc, st_mn, st_mm, st_ssd = st
    b_, l, _ = u.shape
    offs = [sum(IN_SPLITS[:i]) for i in range(1, len(IN_SPLITS))]
    (gq, gk, gv, gg, ga, mq, mk, mv, mo, mi, mfg,
     sx, sz, sb, sc, sdt) = jnp.split(u @ w_in, offs, axis=-1)

    q = gq.reshape(b_, l, GLA_HEADS, GLA_DK) * GLA_DK ** -0.5
    k = gk.reshape(b_, l, GLA_HEADS, GLA_DK)
    v = gv.reshape(b_, l, GLA_HEADS, HEAD_DIM)
    la = jax.nn.log_sigmoid(jnp.einsum('bljr,jrk->bljk', ga.reshape(b_, l, 2, GLA_RANK), gla_wa2)
                            + gla_ba) / GLA_GATE_TEMP
    la = la.reshape(b_, l, 2, GLA_HEADS, GLA_DK)
    o_f, s_f = _gla_scan(q, k, v, la[:, :, 0], st_gla[:, 0])
    o_b, s_b = _gla_scan(_flip(q), _flip(k), _flip(v), _flip(la[:, :, 1]), st_gla[:, 1])
    o_gla = (o_f + _flip(o_b)) * jax.nn.silu(gg).reshape(b_, l, GLA_HEADS, HEAD_DIM)
    new_gla = jnp.stack([s_f, s_b], axis=1)

    q = mq.reshape(b_, l, MLSTM_HEADS, HEAD_DIM)
    k = mk.reshape(b_, l, MLSTM_HEADS, HEAD_DIM) * HEAD_DIM ** -0.5
    v = mv.reshape(b_, l, MLSTM_HEADS, HEAD_DIM)
    ig = mi.reshape(b_, l, 2, MLSTM_HEADS) + mlstm_bi
    lf = jax.nn.log_sigmoid(mfg.reshape(b_, l, 2, MLSTM_HEADS) + mlstm_bf)
    h_f, c_f, n_f, m_f = _mlstm_scan(q, k, v, ig[:, :, 0], lf[:, :, 0],
                                     st_mc[:, 0], st_mn[:, 0], st_mm[:, 0])
    h_b, c_b, n_b, m_b = _mlstm_scan(_flip(q), _flip(k), _flip(v), _flip(ig[:, :, 1]), _flip(lf[:, :, 1]),
                                     st_mc[:, 1], st_mn[:, 1], st_mm[:, 1])
    o_ml = jax.nn.sigmoid(mo).reshape(b_, l, MLSTM_HEADS, HEAD_DIM) * (h_f + _flip(h_b))
    new_mc = jnp.stack([c_f, c_b], axis=1)
    new_mn = jnp.stack([n_f, n_b], axis=1)
    new_mm = jnp.stack([m_f, m_b], axis=1)

    xbc = jax.nn.silu(_centered_dwconv(jnp.concatenate([sx, sb, sc], axis=-1), conv_w, conv_b))
    sx, sb, sc = jnp.split(xbc, [SSD_W, SSD_W + SSD_GN], axis=-1)
    x = sx.reshape(b_, l, SSD_HEADS, HEAD_DIM)
    bm = sb.reshape(b_, l, SSD_GROUPS, SSD_N)
    cm = sc.reshape(b_, l, SSD_GROUPS, SSD_N)
    dt = jax.nn.softplus(sdt.reshape(b_, l, 2, SSD_HEADS) + ssd_dt_bias)
    a = -jnp.exp(ssd_a_log)
    y_f, z_f = _ssd_scan(x, dt[:, :, 0], a[0], bm, cm, st_ssd[:, 0])
    y_b, z_b = _ssd_scan(_flip(x), _flip(dt[:, :, 1]), a[1], _flip(bm), _flip(cm), st_ssd[:, 1])
    y_ssd = (y_f + _flip(y_b) + ssd_d[:, None] * x) * jax.nn.silu(sz).reshape(b_, l, SSD_HEADS, HEAD_DIM)
    new_ssd = jnp.stack([z_f, z_b], axis=1)

    heads = jnp.concatenate([o_gla, o_ml, y_ssd], axis=2)
    heads = _rms_norm(heads) * gnorm_w.reshape(N_MIX_HEADS, HEAD_DIM)
    out = heads.reshape(b_, l, MIX_W) @ w_out
    return out, (new_gla, new_mc, new_mn, new_mm, new_ssd)


def _peer(u, wq, keys, pu, pv):
    b_, l, d = u.shape
    t = b_ * l
    xt = u.reshape(t, d)
    q = (xt @ wq).reshape(t, PEER_HEADS, 2, PEER_DK // 2)
    s = jnp.einsum('thpk,pnk->thpn', q, keys).astype(jnp.float32)
    top_s, top_i = lax.top_k(s, PEER_TOPK)
    cand_s = (top_s[:, :, 0, :, None] + top_s[:, :, 1, None, :]).reshape(t, PEER_HEADS, PEER_TOPK * PEER_TOPK)
    cand_i = (top_i[:, :, 0, :, None] * N_KEYS + top_i[:, :, 1, None, :]).reshape(t, PEER_HEADS, PEER_TOPK * PEER_TOPK)
    best_s, best_j = lax.top_k(cand_s, PEER_TOPK)
    idx = jnp.take_along_axis(cand_i, best_j, axis=-1)
    g = jax.nn.softmax(best_s, axis=-1).astype(u.dtype)
    nblk = t // PEER_BLOCK

    def block(args):
        xb, ib, gb = args
        act = jax.nn.gelu(jnp.einsum('thkd,td->thk', jnp.take(pu, ib, axis=0), xb), approximate=False) * gb
        return jnp.einsum('thk,thkd->td', act, jnp.take(pv, ib, axis=0))

    out = lax.map(block, (xt.reshape(nblk, PEER_BLOCK, d),
                          idx.reshape(nblk, PEER_BLOCK, PEER_HEADS, PEER_TOPK),
                          g.reshape(nblk, PEER_BLOCK, PEER_HEADS, PEER_TOPK)))
    return out.reshape(b_, l, d)


def _layer(x, cond, st, w_ada, b_ada, w_in, conv_w, conv_b, gla_wa2, gla_ba, mlstm_bi, mlstm_bf,
           ssd_dt_bias, ssd_a_log, ssd_d, gnorm_w, w_out, ln1_w, ln1_b,
           peer_wq, peer_keys, peer_u, peer_v, ln2_w, ln2_b):
    mod = jax.nn.silu(cond) @ w_ada + b_ada
    sh1, sc1, g1, sh2, sc2, g2 = jnp.split(mod[:, None, :], 6, axis=-1)
    u = _ln(x) * (1 + sc1) + sh1
    mix, new_st = _mixer(u, st, w_in, conv_w, conv_b, gla_wa2, gla_ba, mlstm_bi, mlstm_bf,
                         ssd_dt_bias, ssd_a_log, ssd_d, gnorm_w, w_out)
    x = _ln(DEEPNORM_ALPHA * x + g1 * mix) * ln1_w + ln1_b
    u = _ln(x) * (1 + sc2) + sh2
    x = _ln(DEEPNORM_ALPHA * x + g2 * _peer(u, peer_wq, peer_keys, peer_u, peer_v)) * ln2_w + ln2_b
    return x, new_st


def setup_inputs(seed: int = 0) -> dict:
    key = jax.random.key(seed)
    ks = jax.random.split(key, 32)
    D = D_MODEL
    f32 = jnp.float32

    def nrm(i, shape):
        return jax.random.normal(ks[i], shape, f32)

    dt0 = jnp.exp(jax.random.uniform(ks[18], (DEPTH, 2, SSD_HEADS), f32,
                                     minval=math.log(1e-3), maxval=math.log(1e-1)))
    return {
        'x_prompt': nrm(0, (BATCH, SEQ, D)),
        'x_sample': nrm(1, (DEC_BATCH, DEC_SEQ, D)),
        'c': nrm(2, (DEC_BATCH, D)),
        'state_gla': 0.5 * nrm(3, (DEC_BATCH, DEPTH, 2, GLA_HEADS, GLA_DK, HEAD_DIM)),
        'state_mlstm_c': 0.3 * nrm(4, (DEC_BATCH, DEPTH, 2, MLSTM_HEADS, HEAD_DIM, HEAD_DIM)),
        'state_mlstm_n': 0.3 * nrm(5, (DEC_BATCH, DEPTH, 2, MLSTM_HEADS, HEAD_DIM)),
        'state_mlstm_m': nrm(6, (DEC_BATCH, DEPTH, 2, MLSTM_HEADS)),
        'state_ssd': 0.5 * nrm(7, (DEC_BATCH, DEPTH, 2, SSD_HEADS, HEAD_DIM, SSD_N)),
        'c_ctx': nrm(8, (D,)),
        'w_ada': nrm(9, (DEPTH, D, 6 * D)) * D ** -0.5,
        'b_ada': 0.02 * nrm(10, (DEPTH, 6 * D)),
        'w_in': nrm(11, (DEPTH, D, IN_W)) * D ** -0.5,
        'conv_w': nrm(12, (DEPTH, CONV_K, CONV_CH)) * CONV_K ** -0.5,
        'conv_b': 0.02 * nrm(13, (DEPTH, CONV_CH)),
        'gla_wa2': nrm(14, (DEPTH, 2, GLA_RANK, GLA_QK)) * GLA_RANK ** -0.5,
        'gla_ba': 0.1 * nrm(15, (DEPTH, 2, GLA_QK)),
        'mlstm_bi': 0.1 * nrm(16, (DEPTH, 2, MLSTM_HEADS)),
        'mlstm_bf': jnp.linspace(3.0, 6.0, MLSTM_HEADS) + 0.1 * nrm(17, (DEPTH, 2, MLSTM_HEADS)),
        'ssd_dt_bias': dt0 + jnp.log(-jnp.expm1(-dt0)),
        'ssd_a_log': jnp.log(jax.random.uniform(ks[19], (DEPTH, 2, SSD_HEADS), f32, minval=1.0, maxval=16.0)),
        'ssd_d': 1.0 + 0.1 * nrm(20, (DEPTH, SSD_HEADS)),
        'gnorm_w': 1.0 + 0.05 * nrm(21, (DEPTH, MIX_W)),
        'w_out': nrm(22, (DEPTH, MIX_W, D)) * (MIX_W ** -0.5 * DEEPNORM_BETA),
        'ln1_w': 1.0 + 0.05 * nrm(23, (DEPTH, D)),
        'ln1_b': 0.02 * nrm(24, (DEPTH, D)),
        'peer_wq': nrm(25, (DEPTH, D, PEER_HEADS * PEER_DK)) * D ** -0.5,
        'peer_keys': nrm(26, (DEPTH, 2, N_KEYS, PEER_DK // 2)) * (PEER_DK // 2) ** -0.5,
        'peer_u': nrm(27, (DEPTH, N_EXPERTS, D)) * D ** -0.5,
        'peer_v': nrm(28, (DEPTH, N_EXPERTS, D)) * DEEPNORM_BETA,
        'ln2_w': 1.0 + 0.05 * nrm(29, (DEPTH, D)),
        'ln2_b': 0.02 * nrm(30, (DEPTH, D)),
    }


def reference(x_prompt, x_sample, c, state_gla, state_mlstm_c, state_mlstm_n, state_mlstm_m, state_ssd,
              c_ctx, w_ada, b_ada, w_in, conv_w, conv_b, gla_wa2, gla_ba, mlstm_bi, mlstm_bf,
              ssd_dt_bias, ssd_a_log, ssd_d, gnorm_w, w_out, ln1_w, ln1_b,
              peer_wq, peer_keys, peer_u, peer_v, ln2_w, ln2_b):
    def layer(h, cond, st, l):
        return _layer(h, cond, st, w_ada[l], b_ada[l], w_in[l], conv_w[l], conv_b[l], gla_wa2[l], gla_ba[l],
                      mlstm_bi[l], mlstm_bf[l], ssd_dt_bias[l], ssd_a_log[l], ssd_d[l], gnorm_w[l], w_out[l],
                      ln1_w[l], ln1_b[l], peer_wq[l], peer_keys[l], peer_u[l], peer_v[l], ln2_w[l], ln2_b[l])

    bp = x_prompt.shape[0]
    dtype = x_prompt.dtype
    zero_st = (jnp.zeros((bp, 2, GLA_HEADS, GLA_DK, HEAD_DIM), dtype),
               jnp.zeros((bp, 2, MLSTM_HEADS, HEAD_DIM, HEAD_DIM), dtype),
               jnp.zeros((bp, 2, MLSTM_HEADS, HEAD_DIM), dtype),
               jnp.zeros((bp, 2, MLSTM_HEADS), dtype),
               jnp.zeros((bp, 2, SSD_HEADS, HEAD_DIM, SSD_N), dtype))
    h = x_prompt
    ctx_states = []
    for l in range(DEPTH):
        h, st = layer(h, c_ctx[None, :], zero_st, l)
        ctx_states.append(st)
    y_prompt = h
    new_state_gla = jnp.stack([s[0] for s in ctx_states], axis=1)
    new_state_mlstm_c = jnp.stack([s[1] for s in ctx_states], axis=1)
    new_state_mlstm_n = jnp.stack([s[2] for s in ctx_states], axis=1)
    new_state_mlstm_m = jnp.stack([s[3] for s in ctx_states], axis=1)
    new_state_ssd = jnp.stack([s[4] for s in ctx_states], axis=1)

    h = x_sample + _grid_pos_embed(x_sample.shape[1], x_sample.shape[2]).astype(x_sample.dtype)
    for l in range(DEPTH):
        st = (state_gla[:, l], state_mlstm_c[:, l], state_mlstm_n[:, l], state_mlstm_m[:, l], state_ssd[:, l])
        h, _ = layer(h, c, st, l)
    y_sample = h

    return (y_prompt, y_sample, new_state_gla, new_state_mlstm_c, new_state_mlstm_n, new_state_mlstm_m, new_state_ssd)
```

```python
import functools
import math

import jax
import jax.numpy as jnp
from jax import lax
from jax.experimental import pallas as pl
from jax.experimental.pallas import tpu as pltpu

F32 = jnp.float32
BF16 = jnp.bfloat16
HI = lax.Precision.HIGHEST

D = 1024
DEPTH = 2
HD = 64
CH = 64
GLA_H, GLA_DK, GLA_RANK = 4, 32, 16
ML_H = 4
SSD_H, SSD_G, SSD_N = 8, 2, 64
CONV_CH = 768
N_KEYS = 128
PEER_H = 8
TOPK = 16
GRID_W = 64

G_W, M_W, S_W = 896, 1152, 1408
IN_PAD = G_W + M_W + S_W

TB = 256
N_CAND = 80
E_CHUNK = 2048
N_ECH = (N_KEYS * N_KEYS) // E_CHUNK
SLAB = TB + 8
VMEM_LIMIT = 56 * 1024 * 1024

NT = (((1,), (1,)), ((), ()))
TN = (((0,), (0,)), ((), ()))


def _const(v):
    return jnp.full((1, 1), v, F32)


def _alpha():
    return lax.sqrt(lax.sqrt(_const(float(2 * DEPTH))))


def _sigmoid(x):
    return 1.0 / (1.0 + jnp.exp(-x))


def _softplus(x):
    return jnp.maximum(x, 0.0) + jnp.log1p(jnp.exp(-jnp.abs(x)))


def _log_sigmoid(x):
    return -_softplus(-x)


def _ln(x, eps=1e-6):
    mu = jnp.mean(x, axis=-1, keepdims=True)
    xc = x - mu
    var = jnp.mean(xc * xc, axis=-1, keepdims=True)
    return xc * lax.rsqrt(var + eps)


def _dot(a, b, dims=None, precision=None):
    if dims is None:
        dims = (((a.ndim - 1,), (0,)), ((), ()))
    return lax.dot_general(a, b, dims, precision=precision, preferred_element_type=F32)


def _bdot(a, b, dims=None):
    return _dot(a.astype(BF16), b.astype(BF16), dims)


def _tri_consts():
    r = lax.broadcasted_iota(jnp.int32, (CH, CH), 0)
    c = lax.broadcasted_iota(jnp.int32, (CH, CH), 1)
    return (r >= c).astype(F32), (r <= c).astype(F32)


def _params(sem, vmem=VMEM_LIMIT):
    return pltpu.CompilerParams(dimension_semantics=sem, vmem_limit_bytes=vmem)


def _ada_kernel(c_ref, w_ref, b_ref, o_ref):
    c = c_ref[...]
    s = c * _sigmoid(c)
    o_ref[0] = _bdot(s, w_ref[0]) + b_ref[0]


def _adaln(cond, w_ada, b_ada):
    rows = cond.shape[0]
    return pl.pallas_call(
        _ada_kernel,
        grid=(DEPTH, 6),
        in_specs=[pl.BlockSpec((rows, D), lambda l, j: (0, 0)),
                  pl.BlockSpec((1, D, D), lambda l, j: (l, 0, j)),
                  pl.BlockSpec((1, 1, D), lambda l, j: (l, 0, j))],
        out_specs=pl.BlockSpec((1, rows, D), lambda l, j: (l, 0, j)),
        out_shape=jax.ShapeDtypeStruct((DEPTH, rows, 6 * D), F32),
        compiler_params=_params(("parallel", "parallel")),
        name="adaln",
    )(cond, w_ada, b_ada.reshape(DEPTH, 1, 6 * D))


def _inproj_kernel(*refs, has_pos):
    if has_pos:
        x_ref, pos_ref, mod_ref, w_ref, xo_ref, og_ref, om_ref, os_ref = refs
    else:
        x_ref, mod_ref, w_ref, og_ref, om_ref, os_ref = refs
    x = x_ref[...]
    if has_pos:
        x = x + pos_ref[...]
        xo_ref[...] = x
    m = mod_ref[...]
    u = _ln(x) * (1.0 + m[1:2]) + m[0:1]
    p = _dot(u.astype(BF16), w_ref[...])
    og_ref[...] = p[:, :G_W]
    om_ref[...] = p[:, G_W:G_W + M_W]
    os_ref[...] = p[:, G_W + M_W:]


def _inproj(x, mod_l, mod_map, w_in, pos):
    t = x.shape[0]
    has_pos = pos is not None
    tok = lambda w: pl.BlockSpec((TB, w), lambda i: (i, 0))
    in_specs = [tok(D)]
    args = [x]
    if has_pos:
        n_pos = pos.shape[0] // TB
        in_specs.append(pl.BlockSpec((TB, D), lambda i: (i % n_pos, 0)))
        args.append(pos)
    in_specs += [pl.BlockSpec((None, 6, D), mod_map),
                 pl.BlockSpec((D, IN_PAD), lambda i: (0, 0))]
    args += [mod_l, w_in]
    out_specs = [tok(G_W), tok(M_W), tok(S_W)]
    out_shape = [jax.ShapeDtypeStruct((t, w), F32) for w in (G_W, M_W, S_W)]
    if has_pos:
        out_specs = [tok(D)] + out_specs
        out_shape = [jax.ShapeDtypeStruct((t, D), F32)] + out_shape
    return pl.pallas_call(
        functools.partial(_inproj_kernel, has_pos=has_pos),
        grid=(t // TB,),
        in_specs=in_specs, out_specs=out_specs, out_shape=out_shape,
        compiler_params=_params(("parallel",)),
        name="inproj",
    )(*args)


def _gla_kernel(*refs, n_chunks, has_init, emit_final):
    it = iter(refs)
    p_ref, wa_ref, ba_ref = next(it), next(it), next(it)
    s0_ref = next(it) if has_init else None
    o_ref = next(it)
    sf_ref = next(it) if emit_final else None
    s_scr, q_scr, b_scr, oi_scr = next(it), next(it), next(it), next(it)

    tril, triu = _tri_consts()
    srow = lax.broadcasted_iota(jnp.int32, (CH, 128), 0)
    blk = (lax.broadcasted_iota(jnp.int32, (128, 256), 0) // GLA_DK
           == lax.broadcasted_iota(jnp.int32, (128, 256), 1) // HD).astype(BF16)
    smask = (lax.broadcasted_iota(jnp.int32, (256, 128), 0) // HD
             == lax.broadcasted_iota(jnp.int32, (256, 128), 1) // GLA_DK).astype(F32)

    if has_init:
        s_scr[...] = s0_ref[...]
    else:
        s_scr[...] = jnp.zeros(s_scr.shape, F32)
    o_ref[...] = jnp.zeros(o_ref.shape, F32)

    def chunk_dir(c, d):
        r0 = pl.multiple_of(c * CH, CH)
        q = p_ref[pl.ds(r0, CH), 0:128] * lax.rsqrt(_const(float(GLA_DK)))
        k = p_ref[pl.ds(r0, CH), 128:256]
        v = p_ref[pl.ds(r0, CH), 256:512]
        gt = p_ref[pl.ds(r0, CH), 768:896]
        z = _dot(gt, wa_ref[d], precision=HI) + ba_ref[d]
        la = _log_sigmoid(z) * (1.0 / 16.0)
        b = _dot(tril if d == 0 else triu, la, precision=HI)
        tot = b[CH - 1:CH] if d == 0 else b[0:1]
        q_scr[d] = q
        b_scr[d] = b

        def grp(g, carry):
            rows = []
            for tt in range(8):
                t = g * 8 + tt
                bt = b_scr[d, pl.ds(t, 1), :]
                qt = q_scr[d, pl.ds(t, 1), :]
                m = (srow <= t) if d == 0 else (srow >= t)
                e = jnp.exp(jnp.where(m, bt - b, -jnp.inf))
                rows.append(k * e * qt)
            pm = jnp.concatenate(rows, axis=0)
            a = _bdot(pm, blk)
            o8 = jnp.sum(a.reshape(8, CH, 256) * v[None], axis=1)
            oi_scr[d, pl.ds(pl.multiple_of(g * 8, 8), 8), :] = o8
            return carry

        lax.fori_loop(0, 8, grp, 0)
        st = s_scr[d]
        o_inter = _bdot(q * jnp.exp(b), st, NT)
        upd = _bdot(v, k * jnp.exp(tot - b), TN)
        s_scr[d] = jnp.exp(tot) * st + upd * smask
        o_ref[pl.ds(r0, CH), :] += oi_scr[d] + o_inter

    def body(j, carry):
        chunk_dir(j, 0)
        chunk_dir(n_chunks - 1 - j, 1)
        return carry

    lax.fori_loop(0, n_chunks, body, 0)
    g = p_ref[:, 512:768]
    o_ref[...] = o_ref[...] * (g * _sigmoid(g))
    if emit_final:
        sf_ref[...] = s_scr[...]


def _gla_scan(pg, wa, ba, s0, n_seq, seq_len):
    has_init = s0 is not None
    emit_final = not has_init
    full = lambda shp: pl.BlockSpec(shp, lambda b: (0,) * len(shp))
    st_spec = pl.BlockSpec((None, 2, 256, 128), lambda b: (b, 0, 0, 0))
    in_specs = [pl.BlockSpec((seq_len, G_W), lambda b: (b, 0)), full((2, 128, 128)), full((2, 1, 128))]
    args = [pg, wa, ba]
    if has_init:
        in_specs.append(st_spec)
        args.append(s0)
    out_specs = [pl.BlockSpec((seq_len, 256), lambda b: (b, 0))]
    out_shape = [jax.ShapeDtypeStruct((n_seq * seq_len, 256), F32)]
    if emit_final:
        out_specs.append(st_spec)
        out_shape.append(jax.ShapeDtypeStruct((n_seq, 2, 256, 128), F32))
    res = pl.pallas_call(
        functools.partial(_gla_kernel, n_chunks=seq_len // CH, has_init=has_init, emit_final=emit_final),
        grid=(n_seq,),
        in_specs=in_specs, out_specs=out_specs, out_shape=out_shape,
        scratch_shapes=[pltpu.VMEM((2, 256, 128), F32), pltpu.VMEM((2, CH, 128), F32),
                        pltpu.VMEM((2, CH, 128), F32), pltpu.VMEM((2, CH, 256), F32)],
        compiler_params=_params(("parallel",)),
        name="gla_scan",
    )(*args)
    return res[0], (res[1] if emit_final else None)


def _seg_consts(width):
    row = lax.broadcasted_iota(jnp.int32, (CH, width), 0)
    lane = lax.broadcasted_iota(jnp.int32, (CH, width), 1) % CH
    return (row == lane).astype(F32), lane <= row, lane >= row


def _mlstm_kernel(*refs, n_chunks, has_init, emit_final):
    it = iter(refs)
    p_ref, ei_ref, ef_ref, bi_ref, bf_ref = (next(it) for _ in range(5))
    if has_init:
        c0_ref, n0_ref, m0_ref = next(it), next(it), next(it)
    o_ref = next(it)
    if emit_final:
        cf_ref, nf_ref, mf_ref = next(it), next(it), next(it)
    c_scr, n_scr, m_scr = next(it), next(it), next(it)

    tril, triu = _tri_consts()
    ones = jnp.ones((CH, CH), F32)
    iexp, mask_f, mask_b = _seg_consts(256)

    if has_init:
        c_scr[...] = c0_ref[...]
        n_scr[...] = n0_ref[...]
        m_scr[...] = m0_ref[...]
    else:
        c_scr[...] = jnp.zeros(c_scr.shape, F32)
        n_scr[...] = jnp.zeros(n_scr.shape, F32)
        m_scr[...] = jnp.zeros(m_scr.shape, F32)
    o_ref[...] = jnp.zeros(o_ref.shape, F32)

    def chunk_dir(c, d):
        r0 = pl.multiple_of(c * CH, CH)
        q = p_ref[pl.ds(r0, CH), 0:256]
        k = p_ref[pl.ds(r0, CH), 256:512] * (HD ** -0.5)
        v = p_ref[pl.ds(r0, CH), 512:768]
        gt = p_ref[pl.ds(r0, CH), 1024:1152]
        ig = _dot(gt, ei_ref[d], precision=HI) + bi_ref[d]
        lf = _log_sigmoid(_dot(gt, ef_ref[d], precision=HI) + bf_ref[d])
        fc = _dot(tril if d == 0 else triu, lf, precision=HI)
        drow = _dot(ones, (ig - fc) * iexp, precision=HI)
        dmat = jnp.where(mask_f if d == 0 else mask_b, fc + drow, -jnp.inf)
        m_prev = m_scr[d]
        n_prev = n_scr[d]
        rowmax = jnp.concatenate(
            [jnp.broadcast_to(jnp.max(dmat[:, h * CH:(h + 1) * CH], axis=1, keepdims=True), (CH, CH))
             for h in range(ML_H)], axis=1)
        m_t = jnp.maximum(fc + m_prev, rowmax)
        w_state = jnp.exp(fc + m_prev - m_t)
        qk = jnp.concatenate(
            [_bdot(q[:, h * HD:(h + 1) * HD], k[:, h * HD:(h + 1) * HD], NT) for h in range(ML_H)], axis=1)
        w = jnp.exp(dmat - m_t) * qk
        ftot = fc[CH - 1:CH] if d == 0 else fc[0:1]
        dl = ftot - fc + ig
        m_new = jnp.maximum(ftot + m_prev, jnp.max(dl, axis=0, keepdims=True))
        ws = jnp.exp(dl - m_new)
        keep = jnp.exp(ftot + m_prev - m_new)
        outs = []
        for h in range(ML_H):
            sl = slice(h * HD, (h + 1) * HD)
            qh, kh, vh, wh = q[:, sl], k[:, sl], v[:, sl], w[:, sl]
            ct = c_scr[d, h]
            num = w_state[:, sl] * _bdot(qh, ct) + _bdot(wh, vh)
            den = (w_state[:, h * HD:h * HD + 1] * jnp.sum(qh * n_prev[:, sl], axis=1, keepdims=True)
                   + jnp.sum(wh, axis=1, keepdims=True))
            outs.append(num / jnp.maximum(jnp.abs(den), jnp.exp(-m_t[:, h * HD:h * HD + 1])))
            c_scr[d, h] = keep[:, sl] * ct + _bdot(kh, ws[:, sl] * vh, TN)
        n_scr[d] = keep * n_prev + jnp.sum(ws * k, axis=0, keepdims=True)
        m_scr[d] = m_new
        o_ref[pl.ds(r0, CH), :] += jnp.concatenate(outs, axis=1)

    def body(j, carry):
        chunk_dir(j, 0)
        chunk_dir(n_chunks - 1 - j, 1)
        return carry

    lax.fori_loop(0, n_chunks, body, 0)
    g = p_ref[:, 768:1024]
    o_ref[...] = o_ref[...] * _sigmoid(g)
    if emit_final:
        cf_ref[...] = c_scr[...]
        nf_ref[...] = n_scr[...]
        mf_ref[...] = m_scr[...]


def _mlstm_scan(pm, ei, ef, bi, bf, init, n_seq, seq_len):
    has_init = init is not None
    emit_final = not has_init
    full = lambda shp: pl.BlockSpec(shp, lambda b: (0,) * len(shp))
    c_spec = pl.BlockSpec((None, 2, ML_H, HD, HD), lambda b: (b, 0, 0, 0, 0))
    v_spec = pl.BlockSpec((None, 2, 1, 256), lambda b: (b, 0, 0, 0))
    in_specs = [pl.BlockSpec((seq_len, M_W), lambda b: (b, 0)), full((2, 128, 256)), full((2, 128, 256)),
                full((2, 1, 256)), full((2, 1, 256))]
    args = [pm, ei, ef, bi, bf]
    if has_init:
        in_specs += [c_spec, v_spec, v_spec]
        args += list(init)
    out_specs = [pl.BlockSpec((seq_len, 256), lambda b: (b, 0))]
    out_shape = [jax.ShapeDtypeStruct((n_seq * seq_len, 256), F32)]
    if emit_final:
        out_specs += [c_spec, v_spec, v_spec]
        out_shape += [jax.ShapeDtypeStruct((n_seq, 2, ML_H, HD, HD), F32),
                      jax.ShapeDtypeStruct((n_seq, 2, 1, 256), F32),
                      jax.ShapeDtypeStruct((n_seq, 2, 1, 256), F32)]
    res = pl.pallas_call(
        functools.partial(_mlstm_kernel, n_chunks=seq_len // CH, has_init=has_init, emit_final=emit_final),
        grid=(n_seq,),
        in_specs=in_specs, out_specs=out_specs, out_shape=out_shape,
        scratch_shapes=[pltpu.VMEM((2, ML_H, HD, HD), F32), pltpu.VMEM((2, 1, 256), F32),
                        pltpu.VMEM((2, 1, 256), F32)],
        compiler_params=_params(("parallel",)),
        name="mlstm_scan",
    )(*args)
    return res[0], (tuple(res[1:]) if emit_final else None)


def _ssd_kernel(*refs, n_chunks, has_init, emit_final):
    it = iter(refs)
    p_ref, cw_ref, cb_ref, edt_ref, dtb_ref, alog_ref, dd_ref = (next(it) for _ in range(7))
    s0_ref = next(it) if has_init else None
    o_ref = next(it)
    sf_ref = next(it) if emit_final else None
    s_scr, xpad_scr, xc_scr = next(it), next(it), next(it)
    seq_len = n_chunks * CH

    tril, triu = _tri_consts()
    ones = jnp.ones((CH, CH), F32)
    iexp, mask_f, mask_b = _seg_consts(512)

    if has_init:
        s_scr[...] = s0_ref[...]
    else:
        s_scr[...] = jnp.zeros(s_scr.shape, F32)
    o_ref[...] = jnp.zeros(o_ref.shape, F32)

    xpad_scr[0:8, :] = jnp.zeros((8, CONV_CH), F32)
    xpad_scr[seq_len + 8:seq_len + 16, :] = jnp.zeros((8, CONV_CH), F32)
    xpad_scr[8:seq_len + 8, 0:512] = p_ref[:, 0:512]
    xpad_scr[8:seq_len + 8, 512:768] = p_ref[:, 1024:1280]
    for c in range(n_chunks):
        r = c * CH
        acc = (xpad_scr[r + 7:r + 7 + CH, :] * cw_ref[0:1, :]
               + xpad_scr[r + 8:r + 8 + CH, :] * cw_ref[1:2, :]
               + xpad_scr[r + 9:r + 9 + CH, :] * cw_ref[2:3, :]) + cb_ref[...]
        xc_scr[r:r + CH, :] = acc * _sigmoid(acc)

    def chunk_dir(c, d):
        r0 = pl.multiple_of(c * CH, CH)
        x = xc_scr[pl.ds(r0, CH), 0:512]
        bm = xc_scr[pl.ds(r0, CH), 512:640]
        cm = xc_scr[pl.ds(r0, CH), 640:768]
        gt = p_ref[pl.ds(r0, CH), 1280:1408]
        dt = _softplus(_dot(gt, edt_ref[d], precision=HI) + dtb_ref[d])
        la = dt * (-jnp.exp(alog_ref[d]))
        acum = _dot(tril if d == 0 else triu, la, precision=HI)
        arow = _dot(ones, acum * iexp, precision=HI)
        decay = jnp.exp(jnp.where(mask_f if d == 0 else mask_b, acum - arow, -jnp.inf))
        cbs = [_bdot(cm[:, g * SSD_N:(g + 1) * SSD_N], bm[:, g * SSD_N:(g + 1) * SSD_N], NT) for g in range(SSD_G)]
        cb = jnp.concatenate([cbs[h // (SSD_H // SSD_G)] for h in range(SSD_H)], axis=1)
        w = cb * decay
        xdt = x * dt
        y_intra = jnp.concatenate(
            [_bdot(w[:, h * CH:(h + 1) * CH], xdt[:, h * HD:(h + 1) * HD]) for h in range(SSD_H)], axis=1)
        st = s_scr[d]
        y_state = jnp.concatenate(
            [_bdot(cm[:, g * SSD_N:(g + 1) * SSD_N], st[:, g * 256:(g + 1) * 256]) for g in range(SSD_G)], axis=1)
        y = y_intra + jnp.exp(acum) * y_state
        alast = acum[CH - 1:CH] if d == 0 else acum[0:1]
        xt = jnp.exp(alast - acum) * xdt
        upd = jnp.concatenate(
            [_bdot(bm[:, g * SSD_N:(g + 1) * SSD_N], xt[:, g * 256:(g + 1) * 256], TN) for g in range(SSD_G)], axis=1)
        s_scr[d] = jnp.exp(alast) * st + upd
        o_ref[pl.ds(r0, CH), :] += y

    def body(j, carry):
        chunk_dir(j, 0)
        chunk_dir(n_chunks - 1 - j, 1)
        return carry

    lax.fori_loop(0, n_chunks, body, 0)
    z = p_ref[:, 512:1024]
    o_ref[...] = (o_ref[...] + dd_ref[...] * xc_scr[:, 0:512]) * (z * _sigmoid(z))
    if emit_final:
        sf_ref[...] = s_scr[...]


def _ssd_scan(ps, cw, cb, edt, dtb, alog, dd, s0, n_seq, seq_len):
    has_init = s0 is not None
    emit_final = not has_init
    full = lambda shp: pl.BlockSpec(shp, lambda b: (0,) * len(shp))
    st_spec = pl.BlockSpec((None, 2, SSD_N, 512), lambda b: (b, 0, 0, 0))
    in_specs = [pl.BlockSpec((seq_len, S_W), lambda b: (b, 0)), full((3, CONV_CH)), full((1, CONV_CH)),
                full((2, 128, 512)), full((2, 1, 512)), full((2, 1, 512)), full((1, 512))]
    args = [ps, cw, cb, edt, dtb, alog, dd]
    if has_init:
        in_specs.append(st_spec)
        args.append(s0)
    out_specs = [pl.BlockSpec((seq_len, 512), lambda b: (b, 0))]
    out_shape = [jax.ShapeDtypeStruct((n_seq * seq_len, 512), F32)]
    if emit_final:
        out_specs.append(st_spec)
        out_shape.append(jax.ShapeDtypeStruct((n_seq, 2, SSD_N, 512), F32))
    res = pl.pallas_call(
        functools.partial(_ssd_kernel, n_chunks=seq_len // CH, has_init=has_init, emit_final=emit_final),
        grid=(n_seq,),
        in_specs=in_specs, out_specs=out_specs, out_shape=out_shape,
        scratch_shapes=[pltpu.VMEM((2, SSD_N, 512), F32), pltpu.VMEM((seq_len + 16, CONV_CH), F32),
                        pltpu.VMEM((seq_len, CONV_CH), F32)],
        compiler_params=_params(("parallel",)),
        name="ssd_scan",
    )(*args)
    return res[0], (res[1] if emit_final else None)


def _post_kernel(x_ref, og_ref, om_ref, os_ref, mod_ref, gw_ref, bd_ref, wo_ref, l1w_ref, l1b_ref, wq_ref,
                 x1_ref, u2_ref, q_ref):
    heads = jnp.concatenate([og_ref[...], om_ref[...], os_ref[...]], axis=1)
    sq = heads * heads
    hi = sq.astype(BF16)
    lo = (sq - hi.astype(F32)).astype(BF16)
    ms = (_dot(hi, bd_ref[...]) + _dot(lo, bd_ref[...])) * (1.0 / HD)
    hn = heads * lax.rsqrt(ms + 1e-6) * gw_ref[...]
    mix = _dot(hn.astype(BF16), wo_ref[...])
    m = mod_ref[...]
    x1 = _ln(_alpha() * x_ref[...] + m[2:3] * mix) * l1w_ref[...] + l1b_ref[...]
    u2 = (_ln(x1) * (1.0 + m[4:5]) + m[3:4]).astype(BF16)
    x1_ref[...] = x1
    u2_ref[...] = u2
    q_ref[...] = _dot(u2, wq_ref[...])


def _post(x, og, om, os_, mod_l, mod_map, gw, bd, wo, l1w, l1b, wq):
    t = x.shape[0]
    tok = lambda w: pl.BlockSpec((TB, w), lambda i: (i, 0))
    full = lambda shp: pl.BlockSpec(shp, lambda i: (0,) * len(shp))
    return pl.pallas_call(
        _post_kernel,
        grid=(t // TB,),
        in_specs=[tok(D), tok(256), tok(256), tok(512), pl.BlockSpec((None, 6, D), mod_map),
                  full((1, D)), full((D, D)), full((D, D)), full((1, D)), full((1, D)), full((D, D))],
        out_specs=[tok(D), tok(D), tok(D)],
        out_shape=[jax.ShapeDtypeStruct((t, D), F32), jax.ShapeDtypeStruct((t, D), BF16),
                   jax.ShapeDtypeStruct((t, D), F32)],
        compiler_params=_params(("parallel",)),
        name="post_mixer",
    )(x, og, om, os_, mod_l, gw, bd, wo, l1w, l1b, wq)


_CAND_BLOCKS = [(0, 0), (0, 8), (1, 0), (2, 0), (3, 0), (4, 0), (5, 0), (6, 0), (7, 0)]


def _cand_ids():
    ids = []
    for ka, kb0 in _CAND_BLOCKS:
        for o in range(8):
            kb = kb0 + o
            ids.append(ka * TOPK + kb if (ka + 1) * (kb + 1) <= TOPK else -1)
    for ka in range(8, 16):
        ids.append(ka * TOPK)
    return jnp.broadcast_to(jnp.asarray(ids, jnp.int32)[:, None], (N_CAND, TB))


def _route_kernel(q_ref, keys_ref, jid_ref, a_ref, b_ref, w_ref, s_scr, i_scr, cand_scr, ci_scr, best_scr):
    q = q_ref[...]
    krow = lax.broadcasted_iota(jnp.int32, (N_KEYS, TB), 0)
    for p in range(2):
        s = _bdot(keys_ref[p], q[:, p * 64:(p + 1) * 64], NT)
        for kk in range(TOPK):
            m = jnp.max(s, axis=0, keepdims=True)
            ix = jnp.min(jnp.where(s == m, krow, N_KEYS), axis=0, keepdims=True)
            s_scr[p, kk:kk + 1, :] = m
            i_scr[p, kk:kk + 1, :] = ix
            s = jnp.where(krow == ix, -jnp.inf, s)
    s1, s2 = s_scr[0], s_scr[1]
    i1, i2 = i_scr[0], i_scr[1]
    for n, (ka, kb0) in enumerate(_CAND_BLOCKS):
        cand_scr[n * 8:(n + 1) * 8, :] = s1[ka:ka + 1] + s2[kb0:kb0 + 8]
        ci_scr[0, n * 8:(n + 1) * 8, :] = jnp.broadcast_to(i1[ka:ka + 1], (8, TB))
        ci_scr[1, n * 8:(n + 1) * 8, :] = i2[kb0:kb0 + 8]
    cand_scr[72:80, :] = s1[8:16] + s2[0:1]
    ci_scr[0, 72:80, :] = i1[8:16]
    ci_scr[1, 72:80, :] = jnp.broadcast_to(i2[0:1], (8, TB))
    jid = jid_ref[...]
    cand = jnp.where(jid >= 0, cand_scr[...], -jnp.inf)
    c1, c2 = ci_scr[0], ci_scr[1]
    for kk in range(TOPK):
        m = jnp.max(cand, axis=0, keepdims=True)
        jm = jnp.min(jnp.where((cand == m) & (jid >= 0), jid, 1 << 20), axis=0, keepdims=True)
        oh = jid == jm
        best_scr[kk:kk + 1, :] = m
        a_ref[kk:kk + 1, :] = jnp.max(jnp.where(oh, c1, -1), axis=0, keepdims=True).astype(F32)
        b_ref[kk:kk + 1, :] = jnp.max(jnp.where(oh, c2, -1), axis=0, keepdims=True).astype(F32)
        cand = jnp.where(oh, -jnp.inf, cand)
    best = best_scr[...]
    e = jnp.exp(best - best[0:1])
    w_ref[...] = e / jnp.sum(e, axis=0, keepdims=True)


def _route(q, keys, jid):
    t = q.shape[0]
    out = pl.BlockSpec((TOPK, TB), lambda i, h: (h, i))
    return pl.pallas_call(
        _route_kernel,
        grid=(t // TB, PEER_H),
        in_specs=[pl.BlockSpec((TB, 128), lambda i, h: (i, h)),
                  pl.BlockSpec((2, N_KEYS, 64), lambda i, h: (0, 0, 0)),
                  pl.BlockSpec((N_CAND, TB), lambda i, h: (0, 0))],
        out_specs=[out, out, out],
        out_shape=[jax.ShapeDtypeStruct((PEER_H * TOPK, t), F32)] * 3,
        scratch_shapes=[pltpu.VMEM((2, TOPK, TB), F32), pltpu.VMEM((2, TOPK, TB), jnp.int32),
                        pltpu.VMEM((N_CAND, TB), F32), pltpu.VMEM((2, N_CAND, TB), jnp.int32),
                        pltpu.VMEM((TOPK, TB), F32)],
        compiler_params=_params(("parallel", "parallel")),
        name="peer_route",
    )(q, keys, jid)


def _gelu(x):
    return 0.5 * x * (1.0 + lax.erf(x * lax.rsqrt(_const(2.0))))


def _expert_kernel(u_ref, x1_ref, mod_ref, a_ref, b_ref, w_ref, pu_ref, pv_ref, l2w_ref, l2b_ref, y_ref,
                   hs_scr, acc_scr, r_scr):
    s = pl.program_id(1)
    per_chunk = E_CHUNK // N_KEYS

    @pl.when(s < N_ECH)
    def _():
        u = u_ref[...]
        for jj in range(per_chunk // 2):
            h2 = _dot(u, pu_ref[jj * 256:(jj + 1) * 256, :], NT)
            j0 = s * per_chunk + jj * 2
            hs_scr[pl.ds(pl.multiple_of(j0 * SLAB, 8), TB), :] = h2[:, :128]
            hs_scr[pl.ds(pl.multiple_of((j0 + 1) * SLAB, 8), TB), :] = h2[:, 128:]

    @pl.when(s == N_ECH - 1)
    def _():
        r_scr[0] = jnp.transpose(a_ref[...])
        r_scr[1] = jnp.transpose(b_ref[...])
        r_scr[2] = jnp.transpose(w_ref[...])
        sub = lax.broadcasted_iota(jnp.int32, (N_KEYS, 128), 0)

        def tok(t, carry):
            arow = r_scr[0, pl.ds(t, 1), :].astype(jnp.int32)
            brow = r_scr[1, pl.ds(t, 1), :].astype(jnp.int32)
            wrow = r_scr[2, pl.ds(t, 1), :]
            uoh = sub == arow
            vb = jnp.where(sub == brow, 1.0, 0.0).astype(BF16)
            ht = hs_scr[pl.ds(t, N_KEYS, stride=SLAB), :]
            hi = ht.astype(BF16)
            lo = (ht - hi.astype(F32)).astype(BF16)
            m = _dot(hi, vb) + _dot(lo, vb)
            hrow = jnp.sum(jnp.where(uoh, m, 0.0), axis=0, keepdims=True)
            act = _gelu(hrow) * wrow
            uw = jnp.where(uoh, act, 0.0).astype(BF16)
            hs_scr[pl.ds(t, N_KEYS, stride=SLAB), :] = _dot(uw, vb, NT)
            return carry

        lax.fori_loop(0, TB, tok, 0)

    @pl.when(s >= N_ECH)
    def _():
        c = s - N_ECH
        a = jnp.concatenate(
            [hs_scr[pl.ds(pl.multiple_of((c * per_chunk + jj) * SLAB, 8), TB), :] for jj in range(per_chunk)],
            axis=1)
        contrib = _dot(a.astype(BF16), pv_ref[...])

        @pl.when(s == N_ECH)
        def _():
            acc_scr[...] = contrib

        @pl.when(s > N_ECH)
        def _():
            acc_scr[...] += contrib

    @pl.when(s == 2 * N_ECH - 1)
    def _():
        m = mod_ref[...]
        y_ref[...] = _ln(_alpha() * x1_ref[...] + m[5:6] * acc_scr[...]) * l2w_ref[...] + l2b_ref[...]


def _experts(u2, x1, mod_l, mod_map, a, b, w, pu, pv, l2w, l2b):
    t = u2.shape[0]
    tok = lambda wd: pl.BlockSpec((TB, wd), lambda i, s: (i, 0))
    rt = pl.BlockSpec((PEER_H * TOPK, TB), lambda i, s: (0, i))
    full = lambda shp: pl.BlockSpec(shp, lambda i, s: (0,) * len(shp))
    mm = lambda i, s: mod_map(i)
    return pl.pallas_call(
        _expert_kernel,
        grid=(t // TB, 2 * N_ECH),
        in_specs=[tok(D), tok(D), pl.BlockSpec((None, 6, D), mm), rt, rt, rt,
                  pl.BlockSpec((E_CHUNK, D), lambda i, s: (jnp.minimum(s, N_ECH - 1), 0)),
                  pl.BlockSpec((E_CHUNK, D), lambda i, s: (jnp.maximum(s - N_ECH, 0), 0)),
                  full((1, D)), full((1, D))],
        out_specs=tok(D),
        out_shape=jax.ShapeDtypeStruct((t, D), F32),
        scratch_shapes=[pltpu.VMEM((N_KEYS * SLAB, 128), F32), pltpu.VMEM((TB, D), F32),
                        pltpu.VMEM((3, TB, 128), F32)],
        compiler_params=_params(("parallel", "arbitrary")),
        name="peer_experts",
    )(u2, x1, mod_l, a, b, w, pu, pv, l2w, l2b)


def _grid_pos_embed(n_tokens, dim):
    rows = n_tokens // GRID_W
    r = jnp.repeat(jnp.arange(rows, dtype=F32), GRID_W)
    col = jnp.tile(jnp.arange(GRID_W, dtype=F32), rows)
    quarter = dim // 4
    inv_freq = 1.0 / (10000.0 ** (jnp.arange(quarter, dtype=F32) / quarter))
    ang_r = r[:, None] * inv_freq
    ang_c = col[:, None] * inv_freq
    return jnp.concatenate([jnp.sin(ang_r), jnp.cos(ang_r), jnp.sin(ang_c), jnp.cos(ang_c)], axis=-1)


def _expander(col0, n_heads):
    c = jnp.arange(128)[None, :, None]
    d = jnp.arange(2)[:, None, None]
    h = (jnp.arange(n_heads * CH) // CH)[None, None, :]
    return (c == col0 + d * n_heads + h).astype(F32)


def _layer_params(l, w_in, conv_w, conv_b, gla_wa2, gla_ba, mlstm_bi, mlstm_bf, ssd_dt_bias, ssd_a_log, ssd_d,
                  gnorm_w, w_out, ln1_w, ln1_b, peer_wq, peer_keys, peer_u, peer_v, ln2_w, ln2_b):
    w = w_in[l]
    z = lambda n: jnp.zeros((D, n), F32)
    w_pad = jnp.concatenate([w[:, :800], z(G_W - 800), w[:, 800:1840], z(M_W - 1040),
                             w[:, 1840:], z(S_W - 1296)], axis=1).astype(BF16)
    wa = jnp.zeros((2, 128, 128), F32)
    wa = wa.at[0, 0:GLA_RANK].set(gla_wa2[l, 0]).at[1, GLA_RANK:2 * GLA_RANK].set(gla_wa2[l, 1])
    rep = lambda a: jnp.repeat(a, CH, axis=-1)
    return dict(
        w_in=w_pad, wa=wa, ba=gla_ba[l][:, None, :],
        ei=_expander(0, ML_H), ef=_expander(2 * ML_H, ML_H),
        bi=rep(mlstm_bi[l])[:, None, :], bf=rep(mlstm_bf[l])[:, None, :],
        cw=conv_w[l], cb=conv_b[l][None, :],
        edt=_expander(0, SSD_H), dtb=rep(ssd_dt_bias[l])[:, None, :], alog=rep(ssd_a_log[l])[:, None, :],
        dd=rep(ssd_d[l])[None, :],
        gw=gnorm_w[l][None, :], wo=w_out[l].astype(BF16), l1w=ln1_w[l][None, :], l1b=ln1_b[l][None, :],
        wq=peer_wq[l].astype(BF16), keys=peer_keys[l].astype(BF16),
        pu=peer_u[l].astype(BF16), pv=peer_v[l].astype(BF16), l2w=ln2_w[l][None, :], l2b=ln2_b[l][None, :],
    )


def _states_in(l, state_gla, state_mlstm_c, state_mlstm_n, state_mlstm_m, state_ssd):
    b = state_gla.shape[0]
    g = jnp.einsum('bdhkv,hg->bdhvgk', state_gla[:, l], jnp.eye(GLA_H, dtype=F32)).reshape(b, 2, 256, 128)
    c = jnp.swapaxes(state_mlstm_c[:, l], -1, -2)
    n = state_mlstm_n[:, l].reshape(b, 2, 1, 256)
    m = jnp.repeat(state_mlstm_m[:, l], HD, axis=-1).reshape(b, 2, 1, 256)
    s = jnp.transpose(state_ssd[:, l], (0, 1, 4, 2, 3)).reshape(b, 2, SSD_N, 512)
    return g, (c, n, m), s


def _states_out(g, cnm, s):
    b = g.shape[0]
    g6 = g.reshape(b, 2, GLA_H, HD, GLA_H, GLA_DK)
    gla = jnp.stack([jnp.swapaxes(g6[:, :, h, :, h, :], -1, -2) for h in range(GLA_H)], axis=2)
    c, n, m = cnm
    mc = jnp.swapaxes(c, -1, -2)
    mn = n.reshape(b, 2, ML_H, HD)
    mm = m.reshape(b, 2, ML_H, HD)[..., 0]
    ssd = jnp.transpose(s.reshape(b, 2, SSD_N, SSD_H, HD), (0, 1, 3, 4, 2))
    return gla, mc, mn, mm, ssd


def _run_group(x, mods, mod_map, params, consts, n_seq, seq_len, pos, init_states):
    finals = []
    for l in range(DEPTH):
        p = params[l]
        mod_l = mods[l]
        if l == 0 and pos is not None:
            x, pg, pm, ps = _inproj(x, mod_l, mod_map, p['w_in'], pos)
        else:
            pg, pm, ps = _inproj(x, mod_l, mod_map, p['w_in'], None)
        init = init_states[l] if init_states is not None else (None, None, None)
        og, fg = _gla_scan(pg, p['wa'], p['ba'], init[0], n_seq, seq_len)
        om, fm = _mlstm_scan(pm, p['ei'], p['ef'], p['bi'], p['bf'], init[1], n_seq, seq_len)
        os_, fs = _ssd_scan(ps, p['cw'], p['cb'], p['edt'], p['dtb'], p['alog'], p['dd'], init[2], n_seq, seq_len)
        x1, u2, q = _post(x, og, om, os_, mod_l, mod_map, p['gw'], consts['bd'], p['wo'], p['l1w'], p['l1b'], p['wq'])
        a, b, w = _route(q, p['keys'], consts['jid'])
        x = _experts(u2, x1, mod_l, mod_map, a, b, w, p['pu'], p['pv'], p['l2w'], p['l2b'])
        finals.append((fg, fm, fs))
    return x, finals


def kernel(x_prompt, x_sample, c, state_gla, state_mlstm_c, state_mlstm_n, state_mlstm_m, state_ssd, c_ctx,
           w_ada, b_ada, w_in, conv_w, conv_b, gla_wa2, gla_ba, mlstm_bi, mlstm_bf, ssd_dt_bias, ssd_a_log, ssd_d,
           gnorm_w, w_out, ln1_w, ln1_b, peer_wq, peer_keys, peer_u, peer_v, ln2_w, ln2_b):
    bp, lp, _ = x_prompt.shape
    bs, ls, _ = x_sample.shape

    cond = jnp.concatenate([c_ctx[None, :], c, jnp.zeros((16 - 1 - bs, D), F32)], axis=0)
    mods = _adaln(cond, w_ada, b_ada).reshape(DEPTH, 16, 6, D)

    params = [_layer_params(l, w_in, conv_w, conv_b, gla_wa2, gla_ba, mlstm_bi, mlstm_bf, ssd_dt_bias, ssd_a_log,
                            ssd_d, gnorm_w, w_out, ln1_w, ln1_b, peer_wq, peer_keys, peer_u, peer_v, ln2_w, ln2_b)
              for l in range(DEPTH)]
    hid = jnp.arange(D) // HD
    consts = dict(bd=(hid[:, None] == hid[None, :]).astype(BF16), jid=_cand_ids())

    y_prompt, finals = _run_group(
        x_prompt.reshape(bp * lp, D), mods, lambda i: (0, 0, 0), params, consts, bp, lp, None, None)

    blocks_per_seq = ls // TB
    init_states = [_states_in(l, state_gla, state_mlstm_c, state_mlstm_n, state_mlstm_m, state_ssd)
                   for l in range(DEPTH)]
    y_sample, _ = _run_group(
        x_sample.reshape(bs * ls, D), mods, lambda i: (1 + i // blocks_per_seq, 0, 0), params, consts, bs, ls,
        _grid_pos_embed(ls, D), init_states)

    outs = [_states_out(*f) for f in finals]
    new_states = tuple(jnp.stack([outs[l][k] for l in range(DEPTH)], axis=1) for k in range(5))
    return (y_prompt.reshape(bp, lp, D), y_sample.reshape(bs, ls, D)) + new_states
```

```python
import functools
import math

import jax
import jax.numpy as jnp
from jax import lax
from jax.experimental import pallas as pl
from jax.experimental.pallas import tpu as pltpu

F32 = jnp.float32
BF16 = jnp.bfloat16
HI = lax.Precision.HIGHEST

D = 1024
DEPTH = 2
HD = 64
CH = 64
GLA_H, GLA_DK, GLA_RANK = 4, 32, 16
ML_H = 4
SSD_H, SSD_G, SSD_N = 8, 2, 64
CONV_CH = 768
N_KEYS = 128
PEER_H = 8
TOPK = 16
GRID_W = 64

G_W, M_W, S_W = 896, 1152, 1408
IN_PAD = G_W + M_W + S_W

TB = 256
SEQ_BLK = 2
N_CAND = 80
E_CHUNK = 2048
N_ECH = (N_KEYS * N_KEYS) // E_CHUNK
SLAB = TB + 8
TOK_GROUP = 16
VMEM_LIMIT = 56 * 1024 * 1024

NT = (((1,), (1,)), ((), ()))
TN = (((0,), (0,)), ((), ()))


def _const(v):
    return jnp.full((1, 1), v, F32)


def _alpha():
    return lax.sqrt(lax.sqrt(_const(float(2 * DEPTH))))


def _sigmoid(x):
    return 1.0 / (1.0 + jnp.exp(-x))


def _softplus(x):
    return jnp.maximum(x, 0.0) + jnp.log1p(jnp.exp(-jnp.abs(x)))


def _log_sigmoid(x):
    return -_softplus(-x)


def _ln(x, eps=1e-6):
    mu = jnp.mean(x, axis=-1, keepdims=True)
    xc = x - mu
    var = jnp.mean(xc * xc, axis=-1, keepdims=True)
    return xc * lax.rsqrt(var + eps)


def _dot(a, b, dims=None, precision=None):
    if dims is None:
        dims = (((a.ndim - 1,), (0,)), ((), ()))
    return lax.dot_general(a, b, dims, precision=precision, preferred_element_type=F32)


def _bdot(a, b, dims=None):
    return _dot(a.astype(BF16), b.astype(BF16), dims)


def _split3(x):
    hi = x.astype(BF16)
    r = x - hi.astype(F32)
    mid = r.astype(BF16)
    lo = (r - mid.astype(F32)).astype(BF16)
    return hi, mid, lo


def _xdot_l(m01, x):
    return sum(_dot(m01, part) for part in _split3(x))


def _xdot_r(x, m01):
    return sum(_dot(part, m01) for part in _split3(x))


def _tri_consts():
    r = lax.broadcasted_iota(jnp.int32, (CH, CH), 0)
    c = lax.broadcasted_iota(jnp.int32, (CH, CH), 1)
    return (r >= c).astype(F32), (r <= c).astype(F32)


def _params(sem, vmem=VMEM_LIMIT):
    return pltpu.CompilerParams(dimension_semantics=sem, vmem_limit_bytes=vmem)


def _ada_kernel(c_ref, w_ref, b_ref, o_ref):
    c = c_ref[...]
    s = c * _sigmoid(c)
    o_ref[0] = _bdot(s, w_ref[0]) + b_ref[0]


def _adaln(cond, w_ada, b_ada):
    rows = cond.shape[0]
    return pl.pallas_call(
        _ada_kernel,
        grid=(DEPTH, 6),
        in_specs=[pl.BlockSpec((rows, D), lambda l, j: (0, 0)),
                  pl.BlockSpec((1, D, D), lambda l, j: (l, 0, j)),
                  pl.BlockSpec((1, 1, D), lambda l, j: (l, 0, j))],
        out_specs=pl.BlockSpec((1, rows, D), lambda l, j: (l, 0, j)),
        out_shape=jax.ShapeDtypeStruct((DEPTH, rows, 6 * D), F32),
        compiler_params=_params(("parallel", "parallel")),
        name="adaln",
    )(cond, w_ada, b_ada.reshape(DEPTH, 1, 6 * D))


def _inproj_kernel(*refs, has_pos):
    if has_pos:
        x_ref, pos_ref, mod_ref, w_ref, xo_ref, og_ref, om_ref, os_ref = refs
    else:
        x_ref, mod_ref, w_ref, og_ref, om_ref, os_ref = refs
    x = x_ref[...]
    if has_pos:
        x = x + pos_ref[...]
        xo_ref[...] = x
    m = mod_ref[...]
    u = _ln(x) * (1.0 + m[1:2]) + m[0:1]
    p = _dot(u.astype(BF16), w_ref[...])
    og_ref[...] = p[:, :G_W]
    om_ref[...] = p[:, G_W:G_W + M_W]
    os_ref[...] = p[:, G_W + M_W:]


def _inproj(x, mod_l, mod_map, w_in, pos):
    t = x.shape[0]
    has_pos = pos is not None
    tok = lambda w: pl.BlockSpec((TB, w), lambda i: (i, 0))
    in_specs = [tok(D)]
    args = [x]
    if has_pos:
        n_pos = pos.shape[0] // TB
        in_specs.append(pl.BlockSpec((TB, D), lambda i: (i % n_pos, 0)))
        args.append(pos)
    in_specs += [pl.BlockSpec((None, 6, D), mod_map),
                 pl.BlockSpec((D, IN_PAD), lambda i: (0, 0))]
    args += [mod_l, w_in]
    out_specs = [tok(G_W), tok(M_W), tok(S_W)]
    out_shape = [jax.ShapeDtypeStruct((t, w), F32) for w in (G_W, M_W, S_W)]
    if has_pos:
        out_specs = [tok(D)] + out_specs
        out_shape = [jax.ShapeDtypeStruct((t, D), F32)] + out_shape
    return pl.pallas_call(
        functools.partial(_inproj_kernel, has_pos=has_pos),
        grid=(t // TB,),
        in_specs=in_specs, out_specs=out_specs, out_shape=out_shape,
        compiler_params=_params(("parallel",)),
        name="inproj",
    )(*args)


def _gla_kernel(*refs, n_chunks, has_init, emit_final):
    it = iter(refs)
    p_ref, wa_ref, ba_ref = next(it), next(it), next(it)
    s0_ref = next(it) if has_init else None
    o_ref = next(it)
    sf_ref = next(it) if emit_final else None
    s_scr, q_scr, b_scr = next(it), next(it), next(it)
    seq_len = n_chunks * CH

    tril, triu = (m.astype(BF16) for m in _tri_consts())
    srow8 = lax.broadcasted_iota(jnp.int32, (8, 128), 0)
    blk = (lax.broadcasted_iota(jnp.int32, (128, 256), 0) // GLA_DK
           == lax.broadcasted_iota(jnp.int32, (128, 256), 1) // HD).astype(BF16)
    smask = (lax.broadcasted_iota(jnp.int32, (256, 128), 0) // HD
             == lax.broadcasted_iota(jnp.int32, (256, 128), 1) // GLA_DK).astype(F32)

    if has_init:
        s_scr[...] = s0_ref[...]
    else:
        s_scr[...] = jnp.zeros(s_scr.shape, F32)
    o_ref[...] = jnp.zeros(o_ref.shape, F32)

    def chunk_dir(sq, c, d):
        i = sq * 2 + d
        r0 = pl.multiple_of(sq * seq_len + c * CH, CH)
        q = p_ref[pl.ds(r0, CH), 0:128] * lax.rsqrt(_const(float(GLA_DK)))
        k = p_ref[pl.ds(r0, CH), 128:256]
        v = p_ref[pl.ds(r0, CH), 256:512]
        gt = p_ref[pl.ds(r0, CH), 768:896]
        z = _bdot(gt, wa_ref[d]) + ba_ref[d]
        la = _log_sigmoid(z) * (1.0 / 16.0)
        b = _xdot_l(tril if d == 0 else triu, la)
        tot = b[CH - 1:CH] if d == 0 else b[0:1]
        q_scr[i] = q
        b_scr[i] = b
        o_rows = []
        for g in range(CH // 8):
            lo, hi = (0, 8 * (g + 1)) if d == 0 else (8 * g, CH)
            n = hi - lo
            kk, bb = k[lo:hi], b[lo:hi]
            rows = []
            for tt in range(8):
                t = 8 * g + tt
                diff = b_scr[i, t:t + 1, :] - bb
                if d == 0:
                    dm = jnp.where(srow8 <= tt, diff[n - 8:], -jnp.inf)
                    diff = dm if n == 8 else jnp.concatenate([diff[:n - 8], dm], axis=0)
                else:
                    dm = jnp.where(srow8 >= tt, diff[:8], -jnp.inf)
                    diff = dm if n == 8 else jnp.concatenate([dm, diff[8:]], axis=0)
                rows.append(kk * jnp.exp(diff) * q_scr[i, t:t + 1, :])
            a = _bdot(jnp.concatenate(rows, axis=0), blk)
            o_rows.append(jnp.sum(a.reshape(8, n, 256) * v[lo:hi][None], axis=1))
        st = s_scr[sq, d]
        o_inter = _bdot(q * jnp.exp(b), st, NT)
        upd = _bdot(v, k * jnp.exp(tot - b), TN)
        s_scr[sq, d] = jnp.exp(tot) * st + upd * smask
        o_ref[pl.ds(r0, CH), :] += jnp.concatenate(o_rows, axis=0) + o_inter

    def body(j, carry):
        for sq in range(SEQ_BLK):
            chunk_dir(sq, j, 0)
            chunk_dir(sq, n_chunks - 1 - j, 1)
        return carry

    lax.fori_loop(0, n_chunks, body, 0)
    g = p_ref[:, 512:768]
    o_ref[...] = o_ref[...] * (g * _sigmoid(g))
    if emit_final:
        sf_ref[...] = s_scr[...]


def _gla_scan(pg, wa, ba, s0, n_seq, seq_len):
    has_init = s0 is not None
    emit_final = not has_init
    full = lambda shp: pl.BlockSpec(shp, lambda b: (0,) * len(shp))
    st_spec = pl.BlockSpec((SEQ_BLK, 2, 256, 128), lambda b: (b, 0, 0, 0))
    in_specs = [pl.BlockSpec((SEQ_BLK * seq_len, G_W), lambda b: (b, 0)), full((2, 128, 128)), full((2, 1, 128))]
    args = [pg, wa, ba]
    if has_init:
        in_specs.append(st_spec)
        args.append(s0)
    out_specs = [pl.BlockSpec((SEQ_BLK * seq_len, 256), lambda b: (b, 0))]
    out_shape = [jax.ShapeDtypeStruct((n_seq * seq_len, 256), F32)]
    if emit_final:
        out_specs.append(st_spec)
        out_shape.append(jax.ShapeDtypeStruct((n_seq, 2, 256, 128), F32))
    res = pl.pallas_call(
        functools.partial(_gla_kernel, n_chunks=seq_len // CH, has_init=has_init, emit_final=emit_final),
        grid=(n_seq // SEQ_BLK,),
        in_specs=in_specs, out_specs=out_specs, out_shape=out_shape,
        scratch_shapes=[pltpu.VMEM((SEQ_BLK, 2, 256, 128), F32), pltpu.VMEM((2 * SEQ_BLK, CH, 128), F32),
                        pltpu.VMEM((2 * SEQ_BLK, CH, 128), F32)],
        compiler_params=_params(("parallel",)),
        name="gla_scan",
    )(*args)
    return res[0], (res[1] if emit_final else None)


def _seg_consts(width):
    row = lax.broadcasted_iota(jnp.int32, (CH, width), 0)
    lane = lax.broadcasted_iota(jnp.int32, (CH, width), 1) % CH
    return (row == lane).astype(F32), lane <= row, lane >= row


def _mlstm_kernel(*refs, n_chunks, has_init, emit_final):
    it = iter(refs)
    p_ref, ei_ref, ef_ref, bi_ref, bf_ref = (next(it) for _ in range(5))
    if has_init:
        c0_ref, n0_ref, m0_ref = next(it), next(it), next(it)
    o_ref = next(it)
    if emit_final:
        cf_ref, nf_ref, mf_ref = next(it), next(it), next(it)
    c_scr, n_scr, m_scr = next(it), next(it), next(it)
    seq_len = n_chunks * CH

    tril, triu = (m.astype(BF16) for m in _tri_consts())
    ones = jnp.ones((CH, CH), BF16)
    iexp, mask_f, mask_b = _seg_consts(256)

    if has_init:
        c_scr[...] = c0_ref[...]
        n_scr[...] = n0_ref[...]
        m_scr[...] = m0_ref[...]
    else:
        c_scr[...] = jnp.zeros(c_scr.shape, F32)
        n_scr[...] = jnp.zeros(n_scr.shape, F32)
        m_scr[...] = jnp.zeros(m_scr.shape, F32)
    o_ref[...] = jnp.zeros(o_ref.shape, F32)

    def chunk_dir(sq, c, d):
        r0 = pl.multiple_of(sq * seq_len + c * CH, CH)
        q = p_ref[pl.ds(r0, CH), 0:256]
        k = p_ref[pl.ds(r0, CH), 256:512] * (HD ** -0.5)
        v = p_ref[pl.ds(r0, CH), 512:768]
        gt = p_ref[pl.ds(r0, CH), 1024:1152]
        ig = _xdot_r(gt + bi_ref[...], ei_ref[d])
        fc = _xdot_r(_xdot_l(tril if d == 0 else triu, _log_sigmoid(gt + bf_ref[...])), ef_ref[d])
        drow = _xdot_l(ones, (ig - fc) * iexp)
        dmat = jnp.where(mask_f if d == 0 else mask_b, fc + drow, -jnp.inf)
        m_prev = m_scr[sq, d]
        n_prev = n_scr[sq, d]
        rowmax = jnp.concatenate(
            [jnp.broadcast_to(jnp.max(dmat[:, h * CH:(h + 1) * CH], axis=1, keepdims=True), (CH, CH))
             for h in range(ML_H)], axis=1)
        m_t = jnp.maximum(fc + m_prev, rowmax)
        w_state = jnp.exp(fc + m_prev - m_t)
        qk = jnp.concatenate(
            [_bdot(q[:, h * HD:(h + 1) * HD], k[:, h * HD:(h + 1) * HD], NT) for h in range(ML_H)], axis=1)
        w = jnp.exp(dmat - m_t) * qk
        ftot = fc[CH - 1:CH] if d == 0 else fc[0:1]
        dl = ftot - fc + ig
        m_new = jnp.maximum(ftot + m_prev, jnp.max(dl, axis=0, keepdims=True))
        ws = jnp.exp(dl - m_new)
        keep = jnp.exp(ftot + m_prev - m_new)
        outs = []
        for h in range(ML_H):
            sl = slice(h * HD, (h + 1) * HD)
            qh, kh, vh, wh = q[:, sl], k[:, sl], v[:, sl], w[:, sl]
            ct = c_scr[sq, d, h]
            num = w_state[:, sl] * _bdot(qh, ct) + _bdot(wh, vh)
            den = (w_state[:, h * HD:h * HD + 1] * jnp.sum(qh * n_prev[:, sl], axis=1, keepdims=True)
                   + jnp.sum(wh, axis=1, keepdims=True))
            outs.append(num / jnp.maximum(jnp.abs(den), jnp.exp(-m_t[:, h * HD:h * HD + 1])))
            c_scr[sq, d, h] = keep[:, sl] * ct + _bdot(kh, ws[:, sl] * vh, TN)
        n_scr[sq, d] = keep * n_prev + jnp.sum(ws * k, axis=0, keepdims=True)
        m_scr[sq, d] = m_new
        o_ref[pl.ds(r0, CH), :] += jnp.concatenate(outs, axis=1)

    def body(j, carry):
        for sq in range(SEQ_BLK):
            chunk_dir(sq, j, 0)
            chunk_dir(sq, n_chunks - 1 - j, 1)
        return carry

    lax.fori_loop(0, n_chunks, body, 0)
    g = p_ref[:, 768:1024]
    o_ref[...] = o_ref[...] * _sigmoid(g)
    if emit_final:
        cf_ref[...] = c_scr[...]
        nf_ref[...] = n_scr[...]
        mf_ref[...] = m_scr[...]


def _mlstm_scan(pm, ei, ef, bi, bf, init, n_seq, seq_len):
    has_init = init is not None
    emit_final = not has_init
    full = lambda shp: pl.BlockSpec(shp, lambda b: (0,) * len(shp))
    c_spec = pl.BlockSpec((SEQ_BLK, 2, ML_H, HD, HD), lambda b: (b, 0, 0, 0, 0))
    v_spec = pl.BlockSpec((SEQ_BLK, 2, 1, 256), lambda b: (b, 0, 0, 0))
    in_specs = [pl.BlockSpec((SEQ_BLK * seq_len, M_W), lambda b: (b, 0)), full((2, 128, 256)),
                full((2, 128, 256)), full((1, 128)), full((1, 128))]
    args = [pm, ei, ef, bi, bf]
    if has_init:
        in_specs += [c_spec, v_spec, v_spec]
        args += list(init)
    out_specs = [pl.BlockSpec((SEQ_BLK * seq_len, 256), lambda b: (b, 0))]
    out_shape = [jax.ShapeDtypeStruct((n_seq * seq_len, 256), F32)]
    if emit_final:
        out_specs += [c_spec, v_spec, v_spec]
        out_shape += [jax.ShapeDtypeStruct((n_seq, 2, ML_H, HD, HD), F32),
                      jax.ShapeDtypeStruct((n_seq, 2, 1, 256), F32),
                      jax.ShapeDtypeStruct((n_seq, 2, 1, 256), F32)]
    res = pl.pallas_call(
        functools.partial(_mlstm_kernel, n_chunks=seq_len // CH, has_init=has_init, emit_final=emit_final),
        grid=(n_seq // SEQ_BLK,),
        in_specs=in_specs, out_specs=out_specs, out_shape=out_shape,
        scratch_shapes=[pltpu.VMEM((SEQ_BLK, 2, ML_H, HD, HD), F32), pltpu.VMEM((SEQ_BLK, 2, 1, 256), F32),
                        pltpu.VMEM((SEQ_BLK, 2, 1, 256), F32)],
        compiler_params=_params(("parallel",)),
        name="mlstm_scan",
    )(*args)
    return res[0], (tuple(res[1:]) if emit_final else None)


def _ssd_kernel(*refs, n_chunks, has_init, emit_final):
    it = iter(refs)
    p_ref, cw_ref, cb_ref, edt_ref, dtb_ref, alog_ref, dd_ref = (next(it) for _ in range(7))
    s0_ref = next(it) if has_init else None
    o_ref = next(it)
    sf_ref = next(it) if emit_final else None
    s_scr, xpad_scr, xc_scr = next(it), next(it), next(it)
    seq_len = n_chunks * CH

    tril, triu = (m.astype(BF16) for m in _tri_consts())
    ones = jnp.ones((CH, CH), BF16)
    iexp, mask_f, mask_b = _seg_consts(512)

    if has_init:
        s_scr[...] = s0_ref[...]
    else:
        s_scr[...] = jnp.zeros(s_scr.shape, F32)
    o_ref[...] = jnp.zeros(o_ref.shape, F32)

    for sq in range(SEQ_BLK):
        base = sq * seq_len
        xpad_scr[sq, 0:8, :] = jnp.zeros((8, CONV_CH), F32)
        xpad_scr[sq, seq_len + 8:seq_len + 16, :] = jnp.zeros((8, CONV_CH), F32)
        xpad_scr[sq, 8:seq_len + 8, 0:512] = p_ref[base:base + seq_len, 0:512]
        xpad_scr[sq, 8:seq_len + 8, 512:768] = p_ref[base:base + seq_len, 1024:1280]
        for c in range(n_chunks):
            r = c * CH
            acc = (xpad_scr[sq, r + 7:r + 7 + CH, :] * cw_ref[0:1, :]
                   + xpad_scr[sq, r + 8:r + 8 + CH, :] * cw_ref[1:2, :]
                   + xpad_scr[sq, r + 9:r + 9 + CH, :] * cw_ref[2:3, :]) + cb_ref[...]
            xc_scr[base + r:base + r + CH, :] = acc * _sigmoid(acc)

    def chunk_dir(sq, c, d):
        r0 = pl.multiple_of(sq * seq_len + c * CH, CH)
        x = xc_scr[pl.ds(r0, CH), 0:512]
        bm = xc_scr[pl.ds(r0, CH), 512:640]
        cm = xc_scr[pl.ds(r0, CH), 640:768]
        dt_c = _softplus(p_ref[pl.ds(r0, CH), 1280:1408] + dtb_ref[...])
        acum_c = _xdot_l(tril if d == 0 else triu, dt_c * (-jnp.exp(alog_ref[...])))
        dt = _xdot_r(dt_c, edt_ref[d])
        acum = _xdot_r(acum_c, edt_ref[d])
        arow = _xdot_l(ones, acum * iexp)
        decay = jnp.exp(jnp.where(mask_f if d == 0 else mask_b, acum - arow, -jnp.inf))
        cbs = [_bdot(cm[:, g * SSD_N:(g + 1) * SSD_N], bm[:, g * SSD_N:(g + 1) * SSD_N], NT) for g in range(SSD_G)]
        cb = jnp.concatenate([cbs[h // (SSD_H // SSD_G)] for h in range(SSD_H)], axis=1)
        w = cb * decay
        xdt = x * dt
        y_intra = jnp.concatenate(
            [_bdot(w[:, h * CH:(h + 1) * CH], xdt[:, h * HD:(h + 1) * HD]) for h in range(SSD_H)], axis=1)
        st = s_scr[sq, d]
        y_state = jnp.concatenate(
            [_bdot(cm[:, g * SSD_N:(g + 1) * SSD_N], st[:, g * 256:(g + 1) * 256]) for g in range(SSD_G)], axis=1)
        y = y_intra + jnp.exp(acum) * y_state
        alast = acum[CH - 1:CH] if d == 0 else acum[0:1]
        xt = jnp.exp(alast - acum) * xdt
        upd = jnp.concatenate(
            [_bdot(bm[:, g * SSD_N:(g + 1) * SSD_N], xt[:, g * 256:(g + 1) * 256], TN) for g in range(SSD_G)], axis=1)
        s_scr[sq, d] = jnp.exp(alast) * st + upd
        o_ref[pl.ds(r0, CH), :] += y

    def body(j, carry):
        for sq in range(SEQ_BLK):
            chunk_dir(sq, j, 0)
            chunk_dir(sq, n_chunks - 1 - j, 1)
        return carry

    lax.fori_loop(0, n_chunks, body, 0)
    z = p_ref[:, 512:1024]
    o_ref[...] = (o_ref[...] + dd_ref[...] * xc_scr[:, 0:512]) * (z * _sigmoid(z))
    if emit_final:
        sf_ref[...] = s_scr[...]


def _ssd_scan(ps, cw, cb, edt, dtb, alog, dd, s0, n_seq, seq_len):
    has_init = s0 is not None
    emit_final = not has_init
    full = lambda shp: pl.BlockSpec(shp, lambda b: (0,) * len(shp))
    st_spec = pl.BlockSpec((SEQ_BLK, 2, SSD_N, 512), lambda b: (b, 0, 0, 0))
    in_specs = [pl.BlockSpec((SEQ_BLK * seq_len, S_W), lambda b: (b, 0)), full((3, CONV_CH)), full((1, CONV_CH)),
                full((2, 128, 512)), full((1, 128)), full((1, 128)), full((1, 512))]
    args = [ps, cw, cb, edt, dtb, alog, dd]
    if has_init:
        in_specs.append(st_spec)
        args.append(s0)
    out_specs = [pl.BlockSpec((SEQ_BLK * seq_len, 512), lambda b: (b, 0))]
    out_shape = [jax.ShapeDtypeStruct((n_seq * seq_len, 512), F32)]
    if emit_final:
        out_specs.append(st_spec)
        out_shape.append(jax.ShapeDtypeStruct((n_seq, 2, SSD_N, 512), F32))
    res = pl.pallas_call(
        functools.partial(_ssd_kernel, n_chunks=seq_len // CH, has_init=has_init, emit_final=emit_final),
        grid=(n_seq // SEQ_BLK,),
        in_specs=in_specs, out_specs=out_specs, out_shape=out_shape,
        scratch_shapes=[pltpu.VMEM((SEQ_BLK, 2, SSD_N, 512), F32),
                        pltpu.VMEM((SEQ_BLK, seq_len + 16, CONV_CH), F32),
                        pltpu.VMEM((SEQ_BLK * seq_len, CONV_CH), F32)],
        compiler_params=_params(("parallel",)),
        name="ssd_scan",
    )(*args)
    return res[0], (res[1] if emit_final else None)


def _post_kernel(x_ref, og_ref, om_ref, os_ref, mod_ref, gw_ref, bd_ref, wo_ref, l1w_ref, l1b_ref, wq_ref,
                 x1_ref, u2_ref, q_ref):
    heads = jnp.concatenate([og_ref[...], om_ref[...], os_ref[...]], axis=1)
    sq = heads * heads
    hi = sq.astype(BF16)
    lo = (sq - hi.astype(F32)).astype(BF16)
    ms = (_dot(hi, bd_ref[...]) + _dot(lo, bd_ref[...])) * (1.0 / HD)
    hn = heads * lax.rsqrt(ms + 1e-6) * gw_ref[...]
    mix = _dot(hn.astype(BF16), wo_ref[...])
    m = mod_ref[...]
    x1 = _ln(_alpha() * x_ref[...] + m[2:3] * mix) * l1w_ref[...] + l1b_ref[...]
    u2 = (_ln(x1) * (1.0 + m[4:5]) + m[3:4]).astype(BF16)
    x1_ref[...] = x1
    u2_ref[...] = u2
    q_ref[...] = _dot(u2, wq_ref[...])


def _post(x, og, om, os_, mod_l, mod_map, gw, bd, wo, l1w, l1b, wq):
    t = x.shape[0]
    tok = lambda w: pl.BlockSpec((TB, w), lambda i: (i, 0))
    full = lambda shp: pl.BlockSpec(shp, lambda i: (0,) * len(shp))
    return pl.pallas_call(
        _post_kernel,
        grid=(t // TB,),
        in_specs=[tok(D), tok(256), tok(256), tok(512), pl.BlockSpec((None, 6, D), mod_map),
                  full((1, D)), full((D, D)), full((D, D)), full((1, D)), full((1, D)), full((D, D))],
        out_specs=[tok(D), tok(D), tok(D)],
        out_shape=[jax.ShapeDtypeStruct((t, D), F32), jax.ShapeDtypeStruct((t, D), BF16),
                   jax.ShapeDtypeStruct((t, D), F32)],
        compiler_params=_params(("parallel",)),
        name="post_mixer",
    )(x, og, om, os_, mod_l, gw, bd, wo, l1w, l1b, wq)


_CAND_BLOCKS = [(0, 0), (0, 8), (1, 0), (2, 0), (3, 0), (4, 0), (5, 0), (6, 0), (7, 0)]


def _cand_ids():
    ids = []
    for ka, kb0 in _CAND_BLOCKS:
        for o in range(8):
            kb = kb0 + o
            ids.append(ka * TOPK + kb if (ka + 1) * (kb + 1) <= TOPK else -1)
    for ka in range(8, 16):
        ids.append(ka * TOPK)
    return jnp.broadcast_to(jnp.asarray(ids, F32)[:, None], (N_CAND, TB))


def _route_kernel(q_ref, keys_ref, jid_ref, a_ref, b_ref, w_ref, s_scr, i_scr, cand_scr, best_scr, j_scr):
    q = q_ref[...]
    krow = lax.broadcasted_iota(jnp.int32, (N_KEYS, TB), 0).astype(F32)
    for p in range(2):
        s = _bdot(keys_ref[p], q[:, p * 64:(p + 1) * 64], NT)
        for kk in range(TOPK):
            m = jnp.max(s, axis=0, keepdims=True)
            ix = jnp.min(jnp.where(s == m, krow, float(N_KEYS)), axis=0, keepdims=True)
            s_scr[p, kk:kk + 1, :] = m
            i_scr[p, kk:kk + 1, :] = ix
            s = jnp.where(krow == ix, -jnp.inf, s)
    s1, s2 = s_scr[0], s_scr[1]
    for n, (ka, kb0) in enumerate(_CAND_BLOCKS):
        cand_scr[n * 8:(n + 1) * 8, :] = s1[ka:ka + 1] + s2[kb0:kb0 + 8]
    cand_scr[72:80, :] = s1[8:16] + s2[0:1]
    jid = jid_ref[...]
    cand = jnp.where(jid >= 0.0, cand_scr[...], -jnp.inf)
    for kk in range(TOPK):
        m = jnp.max(cand, axis=0, keepdims=True)
        jm = jnp.min(jnp.where(cand == m, jid, float(1 << 20)), axis=0, keepdims=True)
        best_scr[kk:kk + 1, :] = m
        j_scr[kk:kk + 1, :] = jm
        cand = jnp.where(jid == jm, -jnp.inf, cand)
    best = best_scr[...]
    e = jnp.exp(best - best[0:1])
    w_ref[...] = e / jnp.sum(e, axis=0, keepdims=True)
    jsel = j_scr[...]
    ka = jnp.floor(jsel * (1.0 / TOPK))
    kb = jsel - ka * TOPK
    i1, i2 = i_scr[0], i_scr[1]
    a = jnp.zeros((TOPK, TB), F32)
    b = jnp.zeros((TOPK, TB), F32)
    for r in range(TOPK):
        a = jnp.where(ka == float(r), i1[r:r + 1], a)
        b = jnp.where(kb == float(r), i2[r:r + 1], b)
    a_ref[...] = a
    b_ref[...] = b


def _route(q, keys, jid):
    t = q.shape[0]
    out = pl.BlockSpec((TOPK, TB), lambda i, h: (h, i))
    return pl.pallas_call(
        _route_kernel,
        grid=(t // TB, PEER_H),
        in_specs=[pl.BlockSpec((TB, 128), lambda i, h: (i, h)),
                  pl.BlockSpec((2, N_KEYS, 64), lambda i, h: (0, 0, 0)),
                  pl.BlockSpec((N_CAND, TB), lambda i, h: (0, 0))],
        out_specs=[out, out, out],
        out_shape=[jax.ShapeDtypeStruct((PEER_H * TOPK, t), F32)] * 3,
        scratch_shapes=[pltpu.VMEM((2, TOPK, TB), F32), pltpu.VMEM((2, TOPK, TB), F32),
                        pltpu.VMEM((N_CAND, TB), F32), pltpu.VMEM((TOPK, TB), F32),
                        pltpu.VMEM((TOPK, TB), F32)],
        compiler_params=_params(("parallel", "parallel")),
        name="peer_route",
    )(q, keys, jid)


def _gelu(x):
    return 0.5 * x * (1.0 + lax.erf(x * lax.rsqrt(_const(2.0))))


def _expert_kernel(u_ref, x1_ref, mod_ref, a_ref, b_ref, w_ref, pu_ref, pv_ref, l2w_ref, l2b_ref, y_ref,
                   hs_scr, acc_scr, r_scr):
    s = pl.program_id(1)
    per_chunk = E_CHUNK // N_KEYS

    @pl.when(s < N_ECH)
    def _():
        u = u_ref[...]
        for jj in range(per_chunk // 2):
            h2 = _dot(u, pu_ref[jj * 256:(jj + 1) * 256, :], NT)
            j0 = s * per_chunk + jj * 2
            hs_scr[pl.ds(pl.multiple_of(j0 * SLAB, 8), TB), :] = h2[:, :128]
            hs_scr[pl.ds(pl.multiple_of((j0 + 1) * SLAB, 8), TB), :] = h2[:, 128:]

    @pl.when(s == N_ECH - 1)
    def _():
        r_scr[0] = jnp.transpose(a_ref[...])
        r_scr[1] = jnp.transpose(b_ref[...])
        r_scr[2] = jnp.transpose(w_ref[...])
        sub = lax.broadcasted_iota(jnp.int32, (N_KEYS, 128), 0)

        def tok_group(g, carry):
            t0 = g * TOK_GROUP
            tiles = [hs_scr[pl.ds(t0 + i, N_KEYS, stride=SLAB), :] for i in range(TOK_GROUP)]
            outs = []
            for i in range(TOK_GROUP):
                t = t0 + i
                arow = r_scr[0, pl.ds(t, 1), :].astype(jnp.int32)
                brow = r_scr[1, pl.ds(t, 1), :].astype(jnp.int32)
                wrow = r_scr[2, pl.ds(t, 1), :]
                uoh = sub == arow
                vb = jnp.where(sub == brow, 1.0, 0.0).astype(BF16)
                ht = tiles[i]
                hi = ht.astype(BF16)
                lo = (ht - hi.astype(F32)).astype(BF16)
                m2 = _dot(jnp.concatenate([hi, lo], axis=0), vb)
                m = m2[:N_KEYS] + m2[N_KEYS:]
                hrow = jnp.sum(jnp.where(uoh, m, 0.0), axis=0, keepdims=True)
                act = _gelu(hrow) * wrow
                uw = jnp.where(uoh, act, 0.0).astype(BF16)
                outs.append(_dot(uw, vb, NT))
            for i in range(TOK_GROUP):
                hs_scr[pl.ds(t0 + i, N_KEYS, stride=SLAB), :] = outs[i]
            return carry

        lax.fori_loop(0, TB // TOK_GROUP, tok_group, 0)

    @pl.when(s >= N_ECH)
    def _():
        c = s - N_ECH
        a = jnp.concatenate(
            [hs_scr[pl.ds(pl.multiple_of((c * per_chunk + jj) * SLAB, 8), TB), :] for jj in range(per_chunk)],
            axis=1)
        contrib = _dot(a.astype(BF16), pv_ref[...])

        @pl.when(s == N_ECH)
        def _():
            acc_scr[...] = contrib

        @pl.when(s > N_ECH)
        def _():
            acc_scr[...] += contrib

    @pl.when(s == 2 * N_ECH - 1)
    def _():
        m = mod_ref[...]
        y_ref[...] = _ln(_alpha() * x1_ref[...] + m[5:6] * acc_scr[...]) * l2w_ref[...] + l2b_ref[...]


def _experts(u2, x1, mod_l, mod_map, a, b, w, pu, pv, l2w, l2b):
    t = u2.shape[0]
    tok = lambda wd: pl.BlockSpec((TB, wd), lambda i, s: (i, 0))
    rt = pl.BlockSpec((PEER_H * TOPK, TB), lambda i, s: (0, i))
    full = lambda shp: pl.BlockSpec(shp, lambda i, s: (0,) * len(shp))
    mm = lambda i, s: mod_map(i)
    return pl.pallas_call(
        _expert_kernel,
        grid=(t // TB, 2 * N_ECH),
        in_specs=[tok(D), tok(D), pl.BlockSpec((None, 6, D), mm), rt, rt, rt,
                  pl.BlockSpec((E_CHUNK, D), lambda i, s: (jnp.minimum(s, N_ECH - 1), 0)),
                  pl.BlockSpec((E_CHUNK, D), lambda i, s: (jnp.maximum(s - N_ECH, 0), 0)),
                  full((1, D)), full((1, D))],
        out_specs=tok(D),
        out_shape=jax.ShapeDtypeStruct((t, D), F32),
        scratch_shapes=[pltpu.VMEM((N_KEYS * SLAB, 128), F32), pltpu.VMEM((TB, D), F32),
                        pltpu.VMEM((3, TB, 128), F32)],
        compiler_params=_params(("parallel", "arbitrary")),
        name="peer_experts",
    )(u2, x1, mod_l, a, b, w, pu, pv, l2w, l2b)


def _grid_pos_embed(n_tokens, dim):
    rows = n_tokens // GRID_W
    r = jnp.repeat(jnp.arange(rows, dtype=F32), GRID_W)
    col = jnp.tile(jnp.arange(GRID_W, dtype=F32), rows)
    quarter = dim // 4
    inv_freq = 1.0 / (10000.0 ** (jnp.arange(quarter, dtype=F32) / quarter))
    ang_r = r[:, None] * inv_freq
    ang_c = col[:, None] * inv_freq
    return jnp.concatenate([jnp.sin(ang_r), jnp.cos(ang_r), jnp.sin(ang_c), jnp.cos(ang_c)], axis=-1)


def _expander(col0, n_heads):
    c = jnp.arange(128)[None, :, None]
    d = jnp.arange(2)[:, None, None]
    h = (jnp.arange(n_heads * CH) // CH)[None, None, :]
    return (c == col0 + d * n_heads + h).astype(BF16)


def _layer_params(l, w_in, conv_w, conv_b, gla_wa2, gla_ba, mlstm_bi, mlstm_bf, ssd_dt_bias, ssd_a_log, ssd_d,
                  gnorm_w, w_out, ln1_w, ln1_b, peer_wq, peer_keys, peer_u, peer_v, ln2_w, ln2_b):
    w = w_in[l]
    z = lambda n: jnp.zeros((D, n), F32)
    w_pad = jnp.concatenate([w[:, :800], z(G_W - 800), w[:, 800:1840], z(M_W - 1040),
                             w[:, 1840:], z(S_W - 1296)], axis=1).astype(BF16)
    wa = jnp.zeros((2, 128, 128), F32)
    wa = wa.at[0, 0:GLA_RANK].set(gla_wa2[l, 0]).at[1, GLA_RANK:2 * GLA_RANK].set(gla_wa2[l, 1])
    rep = lambda a: jnp.repeat(a, CH, axis=-1)
    tile = lambda a, col0: jnp.zeros((1, 128), F32).at[0, col0:col0 + a.size].set(a.reshape(-1))
    return dict(
        w_in=w_pad, wa=wa, ba=gla_ba[l][:, None, :],
        ei=_expander(0, ML_H), ef=_expander(2 * ML_H, ML_H),
        bi=tile(mlstm_bi[l], 0), bf=tile(mlstm_bf[l], 2 * ML_H),
        cw=conv_w[l], cb=conv_b[l][None, :],
        edt=_expander(0, SSD_H), dtb=tile(ssd_dt_bias[l], 0), alog=tile(ssd_a_log[l], 0),
        dd=rep(ssd_d[l])[None, :],
        gw=gnorm_w[l][None, :], wo=w_out[l].astype(BF16), l1w=ln1_w[l][None, :], l1b=ln1_b[l][None, :],
        wq=peer_wq[l].astype(BF16), keys=peer_keys[l].astype(BF16),
        pu=peer_u[l].astype(BF16), pv=peer_v[l].astype(BF16), l2w=ln2_w[l][None, :], l2b=ln2_b[l][None, :],
    )


def _states_in(l, state_gla, state_mlstm_c, state_mlstm_n, state_mlstm_m, state_ssd):
    b = state_gla.shape[0]
    g = jnp.einsum('bdhkv,hg->bdhvgk', state_gla[:, l], jnp.eye(GLA_H, dtype=F32)).reshape(b, 2, 256, 128)
    c = jnp.swapaxes(state_mlstm_c[:, l], -1, -2)
    n = state_mlstm_n[:, l].reshape(b, 2, 1, 256)
    m = jnp.repeat(state_mlstm_m[:, l], HD, axis=-1).reshape(b, 2, 1, 256)
    s = jnp.transpose(state_ssd[:, l], (0, 1, 4, 2, 3)).reshape(b, 2, SSD_N, 512)
    return g, (c, n, m), s


def _states_out(g, cnm, s):
    b = g.shape[0]
    g6 = g.reshape(b, 2, GLA_H, HD, GLA_H, GLA_DK)
    gla = jnp.stack([jnp.swapaxes(g6[:, :, h, :, h, :], -1, -2) for h in range(GLA_H)], axis=2)
    c, n, m = cnm
    mc = jnp.swapaxes(c, -1, -2)
    mn = n.reshape(b, 2, ML_H, HD)
    mm = m.reshape(b, 2, ML_H, HD)[..., 0]
    ssd = jnp.transpose(s.reshape(b, 2, SSD_N, SSD_H, HD), (0, 1, 3, 4, 2))
    return gla, mc, mn, mm, ssd


def _run_group(x, mods, mod_map, params, consts, n_seq, seq_len, pos, init_states):
    finals = []
    for l in range(DEPTH):
        p = params[l]
        mod_l = mods[l]
        if l == 0 and pos is not None:
            x, pg, pm, ps = _inproj(x, mod_l, mod_map, p['w_in'], pos)
        else:
            pg, pm, ps = _inproj(x, mod_l, mod_map, p['w_in'], None)
        init = init_states[l] if init_states is not None else (None, None, None)
        og, fg = _gla_scan(pg, p['wa'], p['ba'], init[0], n_seq, seq_len)
        om, fm = _mlstm_scan(pm, p['ei'], p['ef'], p['bi'], p['bf'], init[1], n_seq, seq_len)
        os_, fs = _ssd_scan(ps, p['cw'], p['cb'], p['edt'], p['dtb'], p['alog'], p['dd'], init[2], n_seq, seq_len)
        x1, u2, q = _post(x, og, om, os_, mod_l, mod_map, p['gw'], consts['bd'], p['wo'], p['l1w'], p['l1b'], p['wq'])
        a, b, w = _route(q, p['keys'], consts['jid'])
        x = _experts(u2, x1, mod_l, mod_map, a, b, w, p['pu'], p['pv'], p['l2w'], p['l2b'])
        finals.append((fg, fm, fs))
    return x, finals


def kernel(x_prompt, x_sample, c, state_gla, state_mlstm_c, state_mlstm_n, state_mlstm_m, state_ssd, c_ctx,
           w_ada, b_ada, w_in, conv_w, conv_b, gla_wa2, gla_ba, mlstm_bi, mlstm_bf, ssd_dt_bias, ssd_a_log, ssd_d,
           gnorm_w, w_out, ln1_w, ln1_b, peer_wq, peer_keys, peer_u, peer_v, ln2_w, ln2_b):
    bp, lp, _ = x_prompt.shape
    bs, ls, _ = x_sample.shape

    cond = jnp.concatenate([c_ctx[None, :], c, jnp.zeros((16 - 1 - bs, D), F32)], axis=0)
    mods = _adaln(cond, w_ada, b_ada).reshape(DEPTH, 16, 6, D)

    params = [_layer_params(l, w_in, conv_w, conv_b, gla_wa2, gla_ba, mlstm_bi, mlstm_bf, ssd_dt_bias, ssd_a_log,
                            ssd_d, gnorm_w, w_out, ln1_w, ln1_b, peer_wq, peer_keys, peer_u, peer_v, ln2_w, ln2_b)
              for l in range(DEPTH)]
    hid = jnp.arange(D) // HD
    consts = dict(bd=(hid[:, None] == hid[None, :]).astype(BF16), jid=_cand_ids())

    y_prompt, finals = _run_group(
        x_prompt.reshape(bp * lp, D), mods, lambda i: (0, 0, 0), params, consts, bp, lp, None, None)

    blocks_per_seq = ls // TB
    init_states = [_states_in(l, state_gla, state_mlstm_c, state_mlstm_n, state_mlstm_m, state_ssd)
                   for l in range(DEPTH)]
    y_sample, _ = _run_group(
        x_sample.reshape(bs * ls, D), mods, lambda i: (1 + i // blocks_per_seq, 0, 0), params, consts, bs, ls,
        _grid_pos_embed(ls, D), init_states)

    outs = [_states_out(*f) for f in finals]
    new_states = tuple(jnp.stack([outs[l][k] for l in range(DEPTH)], axis=1) for k in range(5))
    return (y_prompt.reshape(bp, lp, D), y_sample.reshape(bs, ls, D)) + new_states
```

```python
import functools
import math

import jax
import jax.numpy as jnp
from jax import lax
from jax.experimental import pallas as pl
from jax.experimental.pallas import tpu as pltpu

F32 = jnp.float32
BF16 = jnp.bfloat16
HI = lax.Precision.HIGHEST

D = 1024
DEPTH = 2
HD = 64
CH = 64
GLA_H, GLA_DK, GLA_RANK = 4, 32, 16
ML_H = 4
SSD_H, SSD_G, SSD_N = 8, 2, 64
CONV_CH = 768
N_KEYS = 128
PEER_H = 8
TOPK = 16
GRID_W = 64

G_W, M_W, S_W = 896, 1152, 1408
IN_PAD = G_W + M_W + S_W

TB = 256
SCAN_ROWS = 2048
MAX_SEQ_BLK = 2
N_CAND = 80
E_CHUNK = 2048
N_ECH = (N_KEYS * N_KEYS) // E_CHUNK
TBE = 512
SLABP = TBE // 2 + 8
PAIR_GROUP = 8
VMEM_LIMIT = 56 * 1024 * 1024

NT = (((1,), (1,)), ((), ()))
TN = (((0,), (0,)), ((), ()))


def _const(v):
    return jnp.full((1, 1), v, F32)


def _alpha():
    return lax.sqrt(lax.sqrt(_const(float(2 * DEPTH))))


def _sigmoid(x):
    return 1.0 / (1.0 + jnp.exp(-x))


def _softplus(x):
    return jnp.maximum(x, 0.0) + jnp.log1p(jnp.exp(-jnp.abs(x)))


def _log_sigmoid(x):
    return -_softplus(-x)


def _ln(x, eps=1e-6):
    mu = jnp.mean(x, axis=-1, keepdims=True)
    xc = x - mu
    var = jnp.mean(xc * xc, axis=-1, keepdims=True)
    return xc * lax.rsqrt(var + eps)


def _dot(a, b, dims=None, precision=None):
    if dims is None:
        dims = (((a.ndim - 1,), (0,)), ((), ()))
    return lax.dot_general(a, b, dims, precision=precision, preferred_element_type=F32)


def _bdot(a, b, dims=None):
    return _dot(a.astype(BF16), b.astype(BF16), dims)


def _split3(x):
    hi = x.astype(BF16)
    r = x - hi.astype(F32)
    mid = r.astype(BF16)
    lo = (r - mid.astype(F32)).astype(BF16)
    return hi, mid, lo


def _xdot_l(m01, x):
    return sum(_dot(m01, part) for part in _split3(x))


def _xdot_r(x, m01):
    return sum(_dot(part, m01) for part in _split3(x))


def _tri_consts():
    r = lax.broadcasted_iota(jnp.int32, (CH, CH), 0)
    c = lax.broadcasted_iota(jnp.int32, (CH, CH), 1)
    return (r >= c).astype(F32), (r <= c).astype(F32)


def _params(sem, vmem=VMEM_LIMIT):
    return pltpu.CompilerParams(dimension_semantics=sem, vmem_limit_bytes=vmem)


def _ada_kernel(c_ref, w_ref, b_ref, o_ref):
    c = c_ref[...]
    s = c * _sigmoid(c)
    o_ref[0] = _bdot(s, w_ref[0]) + b_ref[0]


def _adaln(cond, w_ada, b_ada):
    rows = cond.shape[0]
    return pl.pallas_call(
        _ada_kernel,
        grid=(DEPTH, 6),
        in_specs=[pl.BlockSpec((rows, D), lambda l, j: (0, 0)),
                  pl.BlockSpec((1, D, D), lambda l, j: (l, 0, j)),
                  pl.BlockSpec((1, 1, D), lambda l, j: (l, 0, j))],
        out_specs=pl.BlockSpec((1, rows, D), lambda l, j: (l, 0, j)),
        out_shape=jax.ShapeDtypeStruct((DEPTH, rows, 6 * D), F32),
        compiler_params=_params(("parallel", "parallel")),
        name="adaln",
    )(cond, w_ada, b_ada.reshape(DEPTH, 1, 6 * D))


def _inproj_kernel(*refs, has_pos):
    if has_pos:
        x_ref, pos_ref, mod_ref, w_ref, xo_ref, og_ref, om_ref, os_ref = refs
    else:
        x_ref, mod_ref, w_ref, og_ref, om_ref, os_ref = refs
    x = x_ref[...]
    if has_pos:
        x = x + pos_ref[...]
        xo_ref[...] = x
    m = mod_ref[...]
    u = _ln(x) * (1.0 + m[1:2]) + m[0:1]
    p = _dot(u.astype(BF16), w_ref[...])
    og_ref[...] = p[:, :G_W]
    om_ref[...] = p[:, G_W:G_W + M_W]
    os_ref[...] = p[:, G_W + M_W:]


def _inproj(x, mod_l, mod_map, w_in, pos):
    t = x.shape[0]
    has_pos = pos is not None
    tok = lambda w: pl.BlockSpec((TB, w), lambda i: (i, 0))
    in_specs = [tok(D)]
    args = [x]
    if has_pos:
        n_pos = pos.shape[0] // TB
        in_specs.append(pl.BlockSpec((TB, D), lambda i: (i % n_pos, 0)))
        args.append(pos)
    in_specs += [pl.BlockSpec((None, 6, D), mod_map),
                 pl.BlockSpec((D, IN_PAD), lambda i: (0, 0))]
    args += [mod_l, w_in]
    out_specs = [tok(G_W), tok(M_W), tok(S_W)]
    out_shape = [jax.ShapeDtypeStruct((t, w), F32) for w in (G_W, M_W, S_W)]
    if has_pos:
        out_specs = [tok(D)] + out_specs
        out_shape = [jax.ShapeDtypeStruct((t, D), F32)] + out_shape
    return pl.pallas_call(
        functools.partial(_inproj_kernel, has_pos=has_pos),
        grid=(t // TB,),
        in_specs=in_specs, out_specs=out_specs, out_shape=out_shape,
        compiler_params=_params(("parallel",)),
        name="inproj",
    )(*args)


def _gla_kernel(*refs, n_chunks, seq_blk, has_init, emit_final):
    it = iter(refs)
    p_ref, wa_ref, ba_ref = next(it), next(it), next(it)
    s0_ref = next(it) if has_init else None
    o_ref = next(it)
    sf_ref = next(it) if emit_final else None
    s_scr, q_scr, b_scr = next(it), next(it), next(it)
    seq_len = n_chunks * CH

    tril, triu = (m.astype(BF16) for m in _tri_consts())
    srow8 = lax.broadcasted_iota(jnp.int32, (8, 128), 0)
    blk = (lax.broadcasted_iota(jnp.int32, (128, 256), 0) // GLA_DK
           == lax.broadcasted_iota(jnp.int32, (128, 256), 1) // HD).astype(BF16)
    smask = (lax.broadcasted_iota(jnp.int32, (256, 128), 0) // HD
             == lax.broadcasted_iota(jnp.int32, (256, 128), 1) // GLA_DK).astype(F32)

    if has_init:
        s_scr[...] = s0_ref[...]
    else:
        s_scr[...] = jnp.zeros(s_scr.shape, F32)
    o_ref[...] = jnp.zeros(o_ref.shape, F32)

    def chunk_dir(sq, c, d):
        i = sq * 2 + d
        r0 = pl.multiple_of(sq * seq_len + c * CH, CH)
        q = p_ref[pl.ds(r0, CH), 0:128] * lax.rsqrt(_const(float(GLA_DK)))
        k = p_ref[pl.ds(r0, CH), 128:256]
        v = p_ref[pl.ds(r0, CH), 256:512]
        gt = p_ref[pl.ds(r0, CH), 768:896]
        z = _bdot(gt, wa_ref[d]) + ba_ref[d]
        la = _log_sigmoid(z) * (1.0 / 16.0)
        b = _xdot_l(tril if d == 0 else triu, la)
        tot = b[CH - 1:CH] if d == 0 else b[0:1]
        q_scr[i] = q
        b_scr[i] = b
        o_rows = []
        for g in range(CH // 8):
            lo, hi = (0, 8 * (g + 1)) if d == 0 else (8 * g, CH)
            n = hi - lo
            kk, bb = k[lo:hi], b[lo:hi]
            rows = []
            for tt in range(8):
                t = 8 * g + tt
                diff = b_scr[i, t:t + 1, :] - bb
                if d == 0:
                    dm = jnp.where(srow8 <= tt, diff[n - 8:], -jnp.inf)
                    diff = dm if n == 8 else jnp.concatenate([diff[:n - 8], dm], axis=0)
                else:
                    dm = jnp.where(srow8 >= tt, diff[:8], -jnp.inf)
                    diff = dm if n == 8 else jnp.concatenate([dm, diff[8:]], axis=0)
                rows.append(kk * jnp.exp(diff) * q_scr[i, t:t + 1, :])
            a = _bdot(jnp.concatenate(rows, axis=0), blk)
            o_rows.append(jnp.sum(a.reshape(8, n, 256) * v[lo:hi][None], axis=1))
        st = s_scr[sq, d]
        o_inter = _bdot(q * jnp.exp(b), st, NT)
        upd = _bdot(v, k * jnp.exp(tot - b), TN)
        s_scr[sq, d] = jnp.exp(tot) * st + upd * smask
        o_ref[pl.ds(r0, CH), :] += jnp.concatenate(o_rows, axis=0) + o_inter

    def body(j, carry):
        for sq in range(seq_blk):
            chunk_dir(sq, j, 0)
            chunk_dir(sq, n_chunks - 1 - j, 1)
        return carry

    lax.fori_loop(0, n_chunks, body, 0)
    g = p_ref[:, 512:768]
    o_ref[...] = o_ref[...] * (g * _sigmoid(g))
    if emit_final:
        sf_ref[...] = s_scr[...]


def _gla_scan(pg, wa, ba, s0, n_seq, seq_len):
    seq_blk = min(MAX_SEQ_BLK, SCAN_ROWS // seq_len)
    has_init = s0 is not None
    emit_final = not has_init
    full = lambda shp: pl.BlockSpec(shp, lambda b: (0,) * len(shp))
    st_spec = pl.BlockSpec((seq_blk, 2, 256, 128), lambda b: (b, 0, 0, 0))
    in_specs = [pl.BlockSpec((seq_blk * seq_len, G_W), lambda b: (b, 0)), full((2, 128, 128)), full((2, 1, 128))]
    args = [pg, wa, ba]
    if has_init:
        in_specs.append(st_spec)
        args.append(s0)
    out_specs = [pl.BlockSpec((seq_blk * seq_len, 256), lambda b: (b, 0))]
    out_shape = [jax.ShapeDtypeStruct((n_seq * seq_len, 256), F32)]
    if emit_final:
        out_specs.append(st_spec)
        out_shape.append(jax.ShapeDtypeStruct((n_seq, 2, 256, 128), F32))
    res = pl.pallas_call(
        functools.partial(_gla_kernel, n_chunks=seq_len // CH, seq_blk=seq_blk, has_init=has_init,
                          emit_final=emit_final),
        grid=(n_seq // seq_blk,),
        in_specs=in_specs, out_specs=out_specs, out_shape=out_shape,
        scratch_shapes=[pltpu.VMEM((seq_blk, 2, 256, 128), F32), pltpu.VMEM((2 * seq_blk, CH, 128), F32),
                        pltpu.VMEM((2 * seq_blk, CH, 128), F32)],
        compiler_params=_params(("parallel",)),
        name="gla_scan",
    )(*args)
    return res[0], (res[1] if emit_final else None)


def _seg_consts(width):
    row = lax.broadcasted_iota(jnp.int32, (CH, width), 0)
    lane = lax.broadcasted_iota(jnp.int32, (CH, width), 1) % CH
    return (row == lane).astype(F32), lane <= row, lane >= row


def _mlstm_kernel(*refs, n_chunks, seq_blk, has_init, emit_final):
    it = iter(refs)
    p_ref, ei_ref, ef_ref, bi_ref, bf_ref = (next(it) for _ in range(5))
    if has_init:
        c0_ref, n0_ref, m0_ref = next(it), next(it), next(it)
    o_ref = next(it)
    if emit_final:
        cf_ref, nf_ref, mf_ref = next(it), next(it), next(it)
    c_scr, n_scr, m_scr = next(it), next(it), next(it)
    seq_len = n_chunks * CH

    tril, triu = (m.astype(BF16) for m in _tri_consts())
    ones = jnp.ones((CH, CH), BF16)
    iexp, mask_f, mask_b = _seg_consts(256)

    if has_init:
        c_scr[...] = c0_ref[...]
        n_scr[...] = n0_ref[...]
        m_scr[...] = m0_ref[...]
    else:
        c_scr[...] = jnp.zeros(c_scr.shape, F32)
        n_scr[...] = jnp.zeros(n_scr.shape, F32)
        m_scr[...] = jnp.zeros(m_scr.shape, F32)
    o_ref[...] = jnp.zeros(o_ref.shape, F32)

    def chunk_dir(sq, c, d):
        r0 = pl.multiple_of(sq * seq_len + c * CH, CH)
        q = p_ref[pl.ds(r0, CH), 0:256]
        k = p_ref[pl.ds(r0, CH), 256:512] * (HD ** -0.5)
        v = p_ref[pl.ds(r0, CH), 512:768]
        gt = p_ref[pl.ds(r0, CH), 1024:1152]
        ig = _xdot_r(gt + bi_ref[...], ei_ref[d])
        fc = _xdot_r(_xdot_l(tril if d == 0 else triu, _log_sigmoid(gt + bf_ref[...])), ef_ref[d])
        drow = _xdot_l(ones, (ig - fc) * iexp)
        dmat = jnp.where(mask_f if d == 0 else mask_b, fc + drow, -jnp.inf)
        m_prev = m_scr[sq, d]
        n_prev = n_scr[sq, d]
        rowmax = jnp.concatenate(
            [jnp.broadcast_to(jnp.max(dmat[:, h * CH:(h + 1) * CH], axis=1, keepdims=True), (CH, CH))
             for h in range(ML_H)], axis=1)
        m_t = jnp.maximum(fc + m_prev, rowmax)
        w_state = jnp.exp(fc + m_prev - m_t)
        qk = jnp.concatenate(
            [_bdot(q[:, h * HD:(h + 1) * HD], k[:, h * HD:(h + 1) * HD], NT) for h in range(ML_H)], axis=1)
        w = jnp.exp(dmat - m_t) * qk
        ftot = fc[CH - 1:CH] if d == 0 else fc[0:1]
        dl = ftot - fc + ig
        m_new = jnp.maximum(ftot + m_prev, jnp.max(dl, axis=0, keepdims=True))
        ws = jnp.exp(dl - m_new)
        keep = jnp.exp(ftot + m_prev - m_new)
        outs = []
        for h in range(ML_H):
            sl = slice(h * HD, (h + 1) * HD)
            qh, kh, vh, wh = q[:, sl], k[:, sl], v[:, sl], w[:, sl]
            ct = c_scr[sq, d, h]
            num = w_state[:, sl] * _bdot(qh, ct) + _bdot(wh, vh)
            den = (w_state[:, h * HD:h * HD + 1] * jnp.sum(qh * n_prev[:, sl], axis=1, keepdims=True)
                   + jnp.sum(wh, axis=1, keepdims=True))
            outs.append(num / jnp.maximum(jnp.abs(den), jnp.exp(-m_t[:, h * HD:h * HD + 1])))
            c_scr[sq, d, h] = keep[:, sl] * ct + _bdot(kh, ws[:, sl] * vh, TN)
        n_scr[sq, d] = keep * n_prev + jnp.sum(ws * k, axis=0, keepdims=True)
        m_scr[sq, d] = m_new
        o_ref[pl.ds(r0, CH), :] += jnp.concatenate(outs, axis=1)

    def body(j, carry):
        for sq in range(seq_blk):
            chunk_dir(sq, j, 0)
            chunk_dir(sq, n_chunks - 1 - j, 1)
        return carry

    lax.fori_loop(0, n_chunks, body, 0)
    g = p_ref[:, 768:1024]
    o_ref[...] = o_ref[...] * _sigmoid(g)
    if emit_final:
        cf_ref[...] = c_scr[...]
        nf_ref[...] = n_scr[...]
        mf_ref[...] = m_scr[...]


def _mlstm_scan(pm, ei, ef, bi, bf, init, n_seq, seq_len):
    seq_blk = min(MAX_SEQ_BLK, SCAN_ROWS // seq_len)
    has_init = init is not None
    emit_final = not has_init
    full = lambda shp: pl.BlockSpec(shp, lambda b: (0,) * len(shp))
    c_spec = pl.BlockSpec((seq_blk, 2, ML_H, HD, HD), lambda b: (b, 0, 0, 0, 0))
    v_spec = pl.BlockSpec((seq_blk, 2, 1, 256), lambda b: (b, 0, 0, 0))
    in_specs = [pl.BlockSpec((seq_blk * seq_len, M_W), lambda b: (b, 0)), full((2, 128, 256)),
                full((2, 128, 256)), full((1, 128)), full((1, 128))]
    args = [pm, ei, ef, bi, bf]
    if has_init:
        in_specs += [c_spec, v_spec, v_spec]
        args += list(init)
    out_specs = [pl.BlockSpec((seq_blk * seq_len, 256), lambda b: (b, 0))]
    out_shape = [jax.ShapeDtypeStruct((n_seq * seq_len, 256), F32)]
    if emit_final:
        out_specs += [c_spec, v_spec, v_spec]
        out_shape += [jax.ShapeDtypeStruct((n_seq, 2, ML_H, HD, HD), F32),
                      jax.ShapeDtypeStruct((n_seq, 2, 1, 256), F32),
                      jax.ShapeDtypeStruct((n_seq, 2, 1, 256), F32)]
    res = pl.pallas_call(
        functools.partial(_mlstm_kernel, n_chunks=seq_len // CH, seq_blk=seq_blk, has_init=has_init,
                          emit_final=emit_final),
        grid=(n_seq // seq_blk,),
        in_specs=in_specs, out_specs=out_specs, out_shape=out_shape,
        scratch_shapes=[pltpu.VMEM((seq_blk, 2, ML_H, HD, HD), F32), pltpu.VMEM((seq_blk, 2, 1, 256), F32),
                        pltpu.VMEM((seq_blk, 2, 1, 256), F32)],
        compiler_params=_params(("parallel",)),
        name="mlstm_scan",
    )(*args)
    return res[0], (tuple(res[1:]) if emit_final else None)


def _ssd_kernel(*refs, n_chunks, seq_blk, has_init, emit_final):
    it = iter(refs)
    p_ref, cw_ref, cb_ref, edt_ref, dtb_ref, alog_ref, dd_ref = (next(it) for _ in range(7))
    s0_ref = next(it) if has_init else None
    o_ref = next(it)
    sf_ref = next(it) if emit_final else None
    s_scr, xpad_scr, xc_scr = next(it), next(it), next(it)
    seq_len = n_chunks * CH

    tril, triu = (m.astype(BF16) for m in _tri_consts())
    ones = jnp.ones((CH, CH), BF16)
    iexp, mask_f, mask_b = _seg_consts(512)

    if has_init:
        s_scr[...] = s0_ref[...]
    else:
        s_scr[...] = jnp.zeros(s_scr.shape, F32)
    o_ref[...] = jnp.zeros(o_ref.shape, F32)

    for sq in range(seq_blk):
        base = sq * seq_len
        xpad_scr[sq, 0:8, :] = jnp.zeros((8, CONV_CH), F32)
        xpad_scr[sq, seq_len + 8:seq_len + 16, :] = jnp.zeros((8, CONV_CH), F32)
        xpad_scr[sq, 8:seq_len + 8, 0:512] = p_ref[base:base + seq_len, 0:512]
        xpad_scr[sq, 8:seq_len + 8, 512:768] = p_ref[base:base + seq_len, 1024:1280]
        for c in range(n_chunks):
            r = c * CH
            acc = (xpad_scr[sq, r + 7:r + 7 + CH, :] * cw_ref[0:1, :]
                   + xpad_scr[sq, r + 8:r + 8 + CH, :] * cw_ref[1:2, :]
                   + xpad_scr[sq, r + 9:r + 9 + CH, :] * cw_ref[2:3, :]) + cb_ref[...]
            xc_scr[base + r:base + r + CH, :] = acc * _sigmoid(acc)

    def chunk_dir(sq, c, d):
        r0 = pl.multiple_of(sq * seq_len + c * CH, CH)
        x = xc_scr[pl.ds(r0, CH), 0:512]
        bm = xc_scr[pl.ds(r0, CH), 512:640]
        cm = xc_scr[pl.ds(r0, CH), 640:768]
        dt_c = _softplus(p_ref[pl.ds(r0, CH), 1280:1408] + dtb_ref[...])
        acum_c = _xdot_l(tril if d == 0 else triu, dt_c * (-jnp.exp(alog_ref[...])))
        dt = _xdot_r(dt_c, edt_ref[d])
        acum = _xdot_r(acum_c, edt_ref[d])
        arow = _xdot_l(ones, acum * iexp)
        decay = jnp.exp(jnp.where(mask_f if d == 0 else mask_b, acum - arow, -jnp.inf))
        cbs = [_bdot(cm[:, g * SSD_N:(g + 1) * SSD_N], bm[:, g * SSD_N:(g + 1) * SSD_N], NT) for g in range(SSD_G)]
        cb = jnp.concatenate([cbs[h // (SSD_H // SSD_G)] for h in range(SSD_H)], axis=1)
        w = cb * decay
        xdt = x * dt
        y_intra = jnp.concatenate(
            [_bdot(w[:, h * CH:(h + 1) * CH], xdt[:, h * HD:(h + 1) * HD]) for h in range(SSD_H)], axis=1)
        st = s_scr[sq, d]
        y_state = jnp.concatenate(
            [_bdot(cm[:, g * SSD_N:(g + 1) * SSD_N], st[:, g * 256:(g + 1) * 256]) for g in range(SSD_G)], axis=1)
        y = y_intra + jnp.exp(acum) * y_state
        alast = acum[CH - 1:CH] if d == 0 else acum[0:1]
        xt = jnp.exp(alast - acum) * xdt
        upd = jnp.concatenate(
            [_bdot(bm[:, g * SSD_N:(g + 1) * SSD_N], xt[:, g * 256:(g + 1) * 256], TN) for g in range(SSD_G)], axis=1)
        s_scr[sq, d] = jnp.exp(alast) * st + upd
        o_ref[pl.ds(r0, CH), :] += y

    def body(j, carry):
        for sq in range(seq_blk):
            chunk_dir(sq, j, 0)
            chunk_dir(sq, n_chunks - 1 - j, 1)
        return carry

    lax.fori_loop(0, n_chunks, body, 0)
    z = p_ref[:, 512:1024]
    o_ref[...] = (o_ref[...] + dd_ref[...] * xc_scr[:, 0:512]) * (z * _sigmoid(z))
    if emit_final:
        sf_ref[...] = s_scr[...]


def _ssd_scan(ps, cw, cb, edt, dtb, alog, dd, s0, n_seq, seq_len):
    seq_blk = min(MAX_SEQ_BLK, SCAN_ROWS // seq_len)
    has_init = s0 is not None
    emit_final = not has_init
    full = lambda shp: pl.BlockSpec(shp, lambda b: (0,) * len(shp))
    st_spec = pl.BlockSpec((seq_blk, 2, SSD_N, 512), lambda b: (b, 0, 0, 0))
    in_specs = [pl.BlockSpec((seq_blk * seq_len, S_W), lambda b: (b, 0)), full((3, CONV_CH)), full((1, CONV_CH)),
                full((2, 128, 512)), full((1, 128)), full((1, 128)), full((1, 512))]
    args = [ps, cw, cb, edt, dtb, alog, dd]
    if has_init:
        in_specs.append(st_spec)
        args.append(s0)
    out_specs = [pl.BlockSpec((seq_blk * seq_len, 512), lambda b: (b, 0))]
    out_shape = [jax.ShapeDtypeStruct((n_seq * seq_len, 512), F32)]
    if emit_final:
        out_specs.append(st_spec)
        out_shape.append(jax.ShapeDtypeStruct((n_seq, 2, SSD_N, 512), F32))
    res = pl.pallas_call(
        functools.partial(_ssd_kernel, n_chunks=seq_len // CH, seq_blk=seq_blk, has_init=has_init,
                          emit_final=emit_final),
        grid=(n_seq // seq_blk,),
        in_specs=in_specs, out_specs=out_specs, out_shape=out_shape,
        scratch_shapes=[pltpu.VMEM((seq_blk, 2, SSD_N, 512), F32),
                        pltpu.VMEM((seq_blk, seq_len + 16, CONV_CH), F32),
                        pltpu.VMEM((seq_blk * seq_len, CONV_CH), F32)],
        compiler_params=_params(("parallel",)),
        name="ssd_scan",
    )(*args)
    return res[0], (res[1] if emit_final else None)


def _post_kernel(x_ref, og_ref, om_ref, os_ref, mod_ref, gw_ref, bd_ref, wo_ref, l1w_ref, l1b_ref, wq_ref,
                 x1_ref, u2_ref, q_ref):
    heads = jnp.concatenate([og_ref[...], om_ref[...], os_ref[...]], axis=1)
    sq = heads * heads
    hi = sq.astype(BF16)
    lo = (sq - hi.astype(F32)).astype(BF16)
    ms = (_dot(hi, bd_ref[...]) + _dot(lo, bd_ref[...])) * (1.0 / HD)
    hn = heads * lax.rsqrt(ms + 1e-6) * gw_ref[...]
    mix = _dot(hn.astype(BF16), wo_ref[...])
    m = mod_ref[...]
    x1 = _ln(_alpha() * x_ref[...] + m[2:3] * mix) * l1w_ref[...] + l1b_ref[...]
    u2 = (_ln(x1) * (1.0 + m[4:5]) + m[3:4]).astype(BF16)
    x1_ref[...] = x1
    u2_ref[...] = u2
    q_ref[...] = _dot(u2, wq_ref[...])


def _post(x, og, om, os_, mod_l, mod_map, gw, bd, wo, l1w, l1b, wq):
    t = x.shape[0]
    tok = lambda w: pl.BlockSpec((TB, w), lambda i: (i, 0))
    full = lambda shp: pl.BlockSpec(shp, lambda i: (0,) * len(shp))
    return pl.pallas_call(
        _post_kernel,
        grid=(t // TB,),
        in_specs=[tok(D), tok(256), tok(256), tok(512), pl.BlockSpec((None, 6, D), mod_map),
                  full((1, D)), full((D, D)), full((D, D)), full((1, D)), full((1, D)), full((D, D))],
        out_specs=[tok(D), tok(D), tok(D)],
        out_shape=[jax.ShapeDtypeStruct((t, D), F32), jax.ShapeDtypeStruct((t, D), BF16),
                   jax.ShapeDtypeStruct((t, D), F32)],
        compiler_params=_params(("parallel",)),
        name="post_mixer",
    )(x, og, om, os_, mod_l, gw, bd, wo, l1w, l1b, wq)


_CAND_BLOCKS = [(0, 0), (0, 8), (1, 0), (2, 0), (3, 0), (4, 0), (5, 0), (6, 0), (7, 0)]


def _cand_ids():
    ids = []
    for ka, kb0 in _CAND_BLOCKS:
        for o in range(8):
            kb = kb0 + o
            ids.append(ka * TOPK + kb if (ka + 1) * (kb + 1) <= TOPK else -1)
    for ka in range(8, 16):
        ids.append(ka * TOPK)
    return jnp.broadcast_to(jnp.asarray(ids, F32)[:, None], (N_CAND, TB))


def _route_kernel(q_ref, keys_ref, jid_ref, a_ref, b_ref, w_ref, s_scr, i_scr, cand_scr, best_scr, j_scr):
    q = q_ref[...]
    krow = lax.broadcasted_iota(jnp.int32, (N_KEYS, TB), 0).astype(F32)
    for p in range(2):
        s = _bdot(keys_ref[p], q[:, p * 64:(p + 1) * 64], NT)
        for kk in range(TOPK):
            m = jnp.max(s, axis=0, keepdims=True)
            ix = jnp.min(jnp.where(s == m, krow, float(N_KEYS)), axis=0, keepdims=True)
            s_scr[p, kk:kk + 1, :] = m
            i_scr[p, kk:kk + 1, :] = ix
            s = jnp.where(krow == ix, -jnp.inf, s)
    s1, s2 = s_scr[0], s_scr[1]
    for n, (ka, kb0) in enumerate(_CAND_BLOCKS):
        cand_scr[n * 8:(n + 1) * 8, :] = s1[ka:ka + 1] + s2[kb0:kb0 + 8]
    cand_scr[72:80, :] = s1[8:16] + s2[0:1]
    jid = jid_ref[...]
    cand = jnp.where(jid >= 0.0, cand_scr[...], -jnp.inf)
    for kk in range(TOPK):
        m = jnp.max(cand, axis=0, keepdims=True)
        jm = jnp.min(jnp.where(cand == m, jid, float(1 << 20)), axis=0, keepdims=True)
        best_scr[kk:kk + 1, :] = m
        j_scr[kk:kk + 1, :] = jm
        cand = jnp.where(jid == jm, -jnp.inf, cand)
    best = best_scr[...]
    e = jnp.exp(best - best[0:1])
    w_ref[...] = e / jnp.sum(e, axis=0, keepdims=True)
    jsel = j_scr[...]
    ka = jnp.floor(jsel * (1.0 / TOPK))
    kb = jsel - ka * TOPK
    i1, i2 = i_scr[0], i_scr[1]
    a = jnp.zeros((TOPK, TB), F32)
    b = jnp.zeros((TOPK, TB), F32)
    for r in range(TOPK):
        a = jnp.where(ka == float(r), i1[r:r + 1], a)
        b = jnp.where(kb == float(r), i2[r:r + 1], b)
    a_ref[...] = a
    b_ref[...] = b


def _route(q, keys, jid):
    t = q.shape[0]
    out = pl.BlockSpec((TOPK, TB), lambda i, h: (h, i))
    return pl.pallas_call(
        _route_kernel,
        grid=(t // TB, PEER_H),
        in_specs=[pl.BlockSpec((TB, 128), lambda i, h: (i, h)),
                  pl.BlockSpec((2, N_KEYS, 64), lambda i, h: (0, 0, 0)),
                  pl.BlockSpec((N_CAND, TB), lambda i, h: (0, 0))],
        out_specs=[out, out, out],
        out_shape=[jax.ShapeDtypeStruct((PEER_H * TOPK, t), F32)] * 3,
        scratch_shapes=[pltpu.VMEM((2, TOPK, TB), F32), pltpu.VMEM((2, TOPK, TB), F32),
                        pltpu.VMEM((N_CAND, TB), F32), pltpu.VMEM((TOPK, TB), F32),
                        pltpu.VMEM((TOPK, TB), F32)],
        compiler_params=_params(("parallel", "parallel")),
        name="peer_route",
    )(q, keys, jid)


def _gelu(x):
    return 0.5 * x * (1.0 + lax.erf(x * lax.rsqrt(_const(2.0))))


def _expert_kernel(u_ref, x1_ref, mod_ref, a_ref, b_ref, w_ref, pu_ref, pv_ref, l2w_ref, l2b_ref, y_ref,
                   ap_scr, acc_scr, r_scr, idx_scr, hsel_scr):
    s = pl.program_id(1)
    per_chunk = E_CHUNK // N_KEYS

    @pl.when(s == 0)
    def _():
        at = jnp.transpose(a_ref[...])
        bt = jnp.transpose(b_ref[...])
        r_scr[0] = at
        r_scr[1] = bt
        r_scr[2] = jnp.transpose(w_ref[...])
        idx_scr[0] = at.astype(jnp.int32)
        idx_scr[1] = bt.astype(jnp.int32)
        hsel_scr[...] = jnp.zeros(hsel_scr.shape, F32)

    @pl.when(s < N_ECH)
    def _():
        u = u_ref[...]
        a_i, b_i = idx_scr[0], idx_scr[1]
        hsel = hsel_scr[...]
        for jj in range(per_chunk // 2):
            h2 = _dot(u, pu_ref[jj * 256:(jj + 1) * 256, :], NT)
            for half in range(2):
                j = s * per_chunk + jj * 2 + half
                picked = jnp.take_along_axis(h2[:, half * 128:(half + 1) * 128], b_i, axis=1)
                hsel = jnp.where(a_i == j, picked, hsel)
        hsel_scr[...] = hsel

    @pl.when(s == N_ECH - 1)
    def _():
        r_scr[2] = _gelu(hsel_scr[...]) * r_scr[2]
        sub = lax.broadcasted_iota(jnp.int32, (N_KEYS, 128), 0).astype(F32)
        r2 = lax.broadcasted_iota(jnp.int32, (2 * N_KEYS, 256), 0)
        c2 = lax.broadcasted_iota(jnp.int32, (2 * N_KEYS, 256), 1)
        key1_of = jnp.where((r2 & 1) == (c2 >> 7), r2 >> 1, -1).astype(F32)

        def pair_group(g, carry):
            outs = []
            for i in range(PAIR_GROUP):
                t = 2 * (g * PAIR_GROUP + i)
                pair = lambda n: jnp.concatenate([r_scr[n, pl.ds(t, 1), :], r_scr[n, pl.ds(t + 1, 1), :]], axis=1)
                lhs = jnp.where(key1_of == pair(0), pair(2), 0.0)
                rhs = jnp.concatenate(
                    [jnp.where(sub == r_scr[1, pl.ds(t + k, 1), :], 1.0, 0.0) for k in range(2)], axis=1)
                outs.append(pltpu.bitcast(_bdot(lhs, rhs, NT).astype(BF16), jnp.uint32))
            for i in range(PAIR_GROUP):
                ap_scr[pl.ds(g * PAIR_GROUP + i, N_KEYS, stride=SLABP), :] = outs[i]
            return carry

        lax.fori_loop(0, TBE // (2 * PAIR_GROUP), pair_group, 0)

    @pl.when(s >= N_ECH)
    def _():
        c = s - N_ECH
        a = jnp.concatenate(
            [pltpu.bitcast(ap_scr[pl.ds(pl.multiple_of((c * per_chunk + jj) * SLABP, 8), TBE // 2), :], BF16)
             for jj in range(per_chunk)], axis=1)
        contrib = _dot(a, pv_ref[...])

        @pl.when(s == N_ECH)
        def _():
            acc_scr[...] = contrib

        @pl.when(s > N_ECH)
        def _():
            acc_scr[...] += contrib

    @pl.when(s == 2 * N_ECH - 1)
    def _():
        m = mod_ref[...]
        y_ref[...] = _ln(_alpha() * x1_ref[...] + m[5:6] * acc_scr[...]) * l2w_ref[...] + l2b_ref[...]


def _experts(u2, x1, mod_l, mod_map, a, b, w, pu, pv, l2w, l2b):
    t = u2.shape[0]
    tok = lambda wd: pl.BlockSpec((TBE, wd), lambda i, s: (i, 0))
    rt = pl.BlockSpec((PEER_H * TOPK, TBE), lambda i, s: (0, i))
    full = lambda shp: pl.BlockSpec(shp, lambda i, s: (0,) * len(shp))
    mm = lambda i, s: mod_map(i * (TBE // TB))
    return pl.pallas_call(
        _expert_kernel,
        grid=(t // TBE, 2 * N_ECH),
        in_specs=[tok(D), tok(D), pl.BlockSpec((None, 6, D), mm), rt, rt, rt,
                  pl.BlockSpec((E_CHUNK, D), lambda i, s: (jnp.minimum(s, N_ECH - 1), 0)),
                  pl.BlockSpec((E_CHUNK, D), lambda i, s: (jnp.maximum(s - N_ECH, 0), 0)),
                  full((1, D)), full((1, D))],
        out_specs=tok(D),
        out_shape=jax.ShapeDtypeStruct((t, D), F32),
        scratch_shapes=[pltpu.VMEM((N_KEYS * SLABP, 128), jnp.uint32), pltpu.VMEM((TBE, D), F32),
                        pltpu.VMEM((3, TBE, 128), F32), pltpu.VMEM((2, TBE, 128), jnp.int32),
                        pltpu.VMEM((TBE, 128), F32)],
        compiler_params=_params(("parallel", "arbitrary")),
        name="peer_experts",
    )(u2, x1, mod_l, a, b, w, pu, pv, l2w, l2b)


def _grid_pos_embed(n_tokens, dim):
    rows = n_tokens // GRID_W
    r = jnp.repeat(jnp.arange(rows, dtype=F32), GRID_W)
    col = jnp.tile(jnp.arange(GRID_W, dtype=F32), rows)
    quarter = dim // 4
    inv_freq = 1.0 / (10000.0 ** (jnp.arange(quarter, dtype=F32) / quarter))
    ang_r = r[:, None] * inv_freq
    ang_c = col[:, None] * inv_freq
    return jnp.concatenate([jnp.sin(ang_r), jnp.cos(ang_r), jnp.sin(ang_c), jnp.cos(ang_c)], axis=-1)


def _expander(col0, n_heads):
    c = jnp.arange(128)[None, :, None]
    d = jnp.arange(2)[:, None, None]
    h = (jnp.arange(n_heads * CH) // CH)[None, None, :]
    return (c == col0 + d * n_heads + h).astype(BF16)


def _layer_params(l, w_in, conv_w, conv_b, gla_wa2, gla_ba, mlstm_bi, mlstm_bf, ssd_dt_bias, ssd_a_log, ssd_d,
                  gnorm_w, w_out, ln1_w, ln1_b, peer_wq, peer_keys, peer_u, peer_v, ln2_w, ln2_b):
    w = w_in[l]
    z = lambda n: jnp.zeros((D, n), F32)
    w_pad = jnp.concatenate([w[:, :800], z(G_W - 800), w[:, 800:1840], z(M_W - 1040),
                             w[:, 1840:], z(S_W - 1296)], axis=1).astype(BF16)
    wa = jnp.zeros((2, 128, 128), F32)
    wa = wa.at[0, 0:GLA_RANK].set(gla_wa2[l, 0]).at[1, GLA_RANK:2 * GLA_RANK].set(gla_wa2[l, 1])
    rep = lambda a: jnp.repeat(a, CH, axis=-1)
    tile = lambda a, col0: jnp.zeros((1, 128), F32).at[0, col0:col0 + a.size].set(a.reshape(-1))
    return dict(
        w_in=w_pad, wa=wa, ba=gla_ba[l][:, None, :],
        ei=_expander(0, ML_H), ef=_expander(2 * ML_H, ML_H),
        bi=tile(mlstm_bi[l], 0), bf=tile(mlstm_bf[l], 2 * ML_H),
        cw=conv_w[l], cb=conv_b[l][None, :],
        edt=_expander(0, SSD_H), dtb=tile(ssd_dt_bias[l], 0), alog=tile(ssd_a_log[l], 0),
        dd=rep(ssd_d[l])[None, :],
        gw=gnorm_w[l][None, :], wo=w_out[l].astype(BF16), l1w=ln1_w[l][None, :], l1b=ln1_b[l][None, :],
        wq=peer_wq[l].astype(BF16), keys=peer_keys[l].astype(BF16),
        pu=peer_u[l].astype(BF16), pv=peer_v[l].astype(BF16), l2w=ln2_w[l][None, :], l2b=ln2_b[l][None, :],
    )


def _states_in(l, state_gla, state_mlstm_c, state_mlstm_n, state_mlstm_m, state_ssd):
    b = state_gla.shape[0]
    g = jnp.einsum('bdhkv,hg->bdhvgk', state_gla[:, l], jnp.eye(GLA_H, dtype=F32)).reshape(b, 2, 256, 128)
    c = jnp.swapaxes(state_mlstm_c[:, l], -1, -2)
    n = state_mlstm_n[:, l].reshape(b, 2, 1, 256)
    m = jnp.repeat(state_mlstm_m[:, l], HD, axis=-1).reshape(b, 2, 1, 256)
    s = jnp.transpose(state_ssd[:, l], (0, 1, 4, 2, 3)).reshape(b, 2, SSD_N, 512)
    return g, (c, n, m), s


def _states_out(g, cnm, s):
    b = g.shape[0]
    g6 = g.reshape(b, 2, GLA_H, HD, GLA_H, GLA_DK)
    gla = jnp.stack([jnp.swapaxes(g6[:, :, h, :, h, :], -1, -2) for h in range(GLA_H)], axis=2)
    c, n, m = cnm
    mc = jnp.swapaxes(c, -1, -2)
    mn = n.reshape(b, 2, ML_H, HD)
    mm = m.reshape(b, 2, ML_H, HD)[..., 0]
    ssd = jnp.transpose(s.reshape(b, 2, SSD_N, SSD_H, HD), (0, 1, 3, 4, 2))
    return gla, mc, mn, mm, ssd


def _run_group(x, mods, mod_map, params, consts, n_seq, seq_len, pos, init_states):
    finals = []
    for l in range(DEPTH):
        p = params[l]
        mod_l = mods[l]
        if l == 0 and pos is not None:
            x, pg, pm, ps = _inproj(x, mod_l, mod_map, p['w_in'], pos)
        else:
            pg, pm, ps = _inproj(x, mod_l, mod_map, p['w_in'], None)
        init = init_states[l] if init_states is not None else (None, None, None)
        og, fg = _gla_scan(pg, p['wa'], p['ba'], init[0], n_seq, seq_len)
        om, fm = _mlstm_scan(pm, p['ei'], p['ef'], p['bi'], p['bf'], init[1], n_seq, seq_len)
        os_, fs = _ssd_scan(ps, p['cw'], p['cb'], p['edt'], p['dtb'], p['alog'], p['dd'], init[2], n_seq, seq_len)
        x1, u2, q = _post(x, og, om, os_, mod_l, mod_map, p['gw'], consts['bd'], p['wo'], p['l1w'], p['l1b'], p['wq'])
        a, b, w = _route(q, p['keys'], consts['jid'])
        x = _experts(u2, x1, mod_l, mod_map, a, b, w, p['pu'], p['pv'], p['l2w'], p['l2b'])
        finals.append((fg, fm, fs))
    return x, finals


def kernel(x_prompt, x_sample, c, state_gla, state_mlstm_c, state_mlstm_n, state_mlstm_m, state_ssd, c_ctx,
           w_ada, b_ada, w_in, conv_w, conv_b, gla_wa2, gla_ba, mlstm_bi, mlstm_bf, ssd_dt_bias, ssd_a_log, ssd_d,
           gnorm_w, w_out, ln1_w, ln1_b, peer_wq, peer_keys, peer_u, peer_v, ln2_w, ln2_b):
    bp, lp, _ = x_prompt.shape
    bs, ls, _ = x_sample.shape

    cond = jnp.concatenate([c_ctx[None, :], c, jnp.zeros((16 - 1 - bs, D), F32)], axis=0)
    mods = _adaln(cond, w_ada, b_ada).reshape(DEPTH, 16, 6, D)

    params = [_layer_params(l, w_in, conv_w, conv_b, gla_wa2, gla_ba, mlstm_bi, mlstm_bf, ssd_dt_bias, ssd_a_log,
                            ssd_d, gnorm_w, w_out, ln1_w, ln1_b, peer_wq, peer_keys, peer_u, peer_v, ln2_w, ln2_b)
              for l in range(DEPTH)]
    hid = jnp.arange(D) // HD
    consts = dict(bd=(hid[:, None] == hid[None, :]).astype(BF16), jid=_cand_ids())

    y_prompt, finals = _run_group(
        x_prompt.reshape(bp * lp, D), mods, lambda i: (0, 0, 0), params, consts, bp, lp, None, None)

    blocks_per_seq = ls // TB
    init_states = [_states_in(l, state_gla, state_mlstm_c, state_mlstm_n, state_mlstm_m, state_ssd)
                   for l in range(DEPTH)]
    y_sample, _ = _run_group(
        x_sample.reshape(bs * ls, D), mods, lambda i: (1 + i // blocks_per_seq, 0, 0), params, consts, bs, ls,
        _grid_pos_embed(ls, D), init_states)

    outs = [_states_out(*f) for f in finals]
    new_states = tuple(jnp.stack([outs[l][k] for l in range(DEPTH)], axis=1) for k in range(5))
    return (y_prompt.reshape(bp, lp, D), y_sample.reshape(bs, ls, D)) + new_states
```
